```python
import jax, jax.numpy as jnp
from jax import lax
import numpy as np

D_MODEL = 2048
BATCH = 8
SEQ = 2048
DEPTH = 1

N_META = 16
HEAD_DIM = 64
MIX_WIDTH = D_MODEL
RWKV_WIDTH = MIX_WIDTH // 2
SB_WIDTH = MIX_WIDTH - RWKV_WIDTH
RWKV_HEADS = RWKV_WIDTH // HEAD_DIM
SB_HEADS = SB_WIDTH // HEAD_DIM
DECAY_LORA = 64
ICLR_LORA = 64
GATE_LORA = 160
RWKV_COLS = 3 * RWKV_WIDTH + DECAY_LORA + ICLR_LORA + GATE_LORA
SB_COLS = 3 * SB_WIDTH
IN_COLS = RWKV_COLS + SB_COLS
D_FF = ((8 * D_MODEL // 3 + 255) // 256) * 256
CONV_WIDTH = 3
SB_BLOCK = 128
DEEPNORM_ALPHA = (2.0 * DEPTH) ** 0.25
DEEPNORM_BETA = (8.0 * DEPTH) ** -0.25
LN_EPS = 1e-5
GN_EPS = 64e-5
RMS_EPS = 1e-6

kernel_name = "hymba_rwkv7_stickbreak_deepnorm_convffn"


def layer_norm(x, g, b):
    xf = x.astype(jnp.float32)
    mu = jnp.mean(xf, axis=-1, keepdims=True)
    var = jnp.mean(jnp.square(xf - mu), axis=-1, keepdims=True)
    return ((xf - mu) * lax.rsqrt(var + LN_EPS) * g + b).astype(x.dtype)


def rwkv7_mixer(p, mu, w0, w2, a0, a2, g2, k_k, k_a, r_k, gn_g, gn_b):
    B, T, _ = p.shape
    C = RWKV_WIDTH
    f32 = jnp.float32
    p_prev = jnp.pad(p, ((0, 0), (1, 0), (0, 0)))[:, :-1]
    p = p + (p_prev - p) * mu
    r, k, v, wl, al, gl = jnp.split(
        p, [C, 2 * C, 3 * C, 3 * C + DECAY_LORA, 3 * C + DECAY_LORA + ICLR_LORA], axis=-1)
    w_log = -jax.nn.softplus(-(w0 + jnp.tanh(wl) @ w2)) - 0.5
    a = jax.nn.sigmoid(a0 + al @ a2)
    g = jax.nn.sigmoid(gl) @ g2
    heads = lambda t: t.reshape(B, T, RWKV_HEADS, HEAD_DIM).astype(f32)
    kk = heads(k * k_k)
    kk = kk / jnp.maximum(jnp.linalg.norm(kk, axis=-1, keepdims=True), 1e-12)
    k = k * (1.0 + (a - 1.0) * k_a)
    r_h, k_h, v_h, a_h = heads(r), heads(k), heads(v), heads(a)
    decay = jnp.exp(-jnp.exp(heads(w_log)))

    def step(S, inp):
        r_t, dec_t, k_t, v_t, kk_t, a_t = inp
        sa = jnp.einsum('bhvk,bhk->bhv', S, -kk_t)
        S = (S * dec_t[:, :, None, :] + sa[..., None] * (kk_t * a_t)[:, :, None, :]
             + v_t[..., None] * k_t[:, :, None, :])
        return S, jnp.einsum('bhvk,bhk->bhv', S, r_t)

    xs = tuple(jnp.moveaxis(t, 1, 0) for t in (r_h, decay, k_h, v_h, kk, a_h))
    S0 = jnp.zeros((B, RWKV_HEADS, HEAD_DIM, HEAD_DIM), f32)
    _, y = lax.scan(step, S0, xs)
    y = jnp.moveaxis(y, 0, 1)
    mean = jnp.mean(y, axis=-1, keepdims=True)
    var = jnp.mean(jnp.square(y - mean), axis=-1, keepdims=True)
    y = ((y - mean) * lax.rsqrt(var + GN_EPS)).reshape(B, T, C) * gn_g + gn_b
    bonus = (jnp.sum(r_h * k_h * r_k, axis=-1, keepdims=True) * v_h).reshape(B, T, C)
    return ((y + bonus) * g).astype(p.dtype)


def stick_breaking_mixer(p, norm_g):
    B, T, _ = p.shape
    f32 = jnp.float32
    to_heads = lambda t: t.reshape(B, T, SB_HEADS, HEAD_DIM).transpose(0, 2, 1, 3)
    q, k, v = (to_heads(t) for t in jnp.split(p, 3, axis=-1))
    scale = HEAD_DIM ** -0.5
    bounds = [0, N_META] + [N_META + SB_BLOCK * (i + 1) for i in range((T - N_META) // SB_BLOCK)]
    outs = []
    for qs, qe in zip(bounds[:-1], bounds[1:]):
        z = jnp.einsum('bhqd,bhkd->bhqk', q[:, :, qs:qe], k[:, :, :qe]).astype(f32) * scale
        strict = jnp.arange(qe)[None, :] < jnp.arange(qs, qe)[:, None]
        log_beta = jax.nn.log_sigmoid(z)
        log_keep = jnp.where(strict, jax.nn.log_sigmoid(-z), 0.0)
        after = jnp.sum(log_keep, axis=-1, keepdims=True) - jnp.cumsum(log_keep, axis=-1)
        att = jnp.where(strict, jnp.exp(log_beta + after), 0.0)
        outs.append(jnp.einsum('bhqk,bhkd->bhqd', att, v[:, :, :qe].astype(f32)))
    o = jnp.concatenate(outs, axis=2).transpose(0, 2, 1, 3)
    o = o * lax.rsqrt(jnp.mean(jnp.square(o), axis=-1, keepdims=True) + RMS_EPS)
    return (o.reshape(B, T, SB_WIDTH) * norm_g).astype(p.dtype)


def conv_ffn(h, w_up, conv_w, conv_b, w_down):
    gate, val = jnp.split(h @ w_up, 2, axis=-1)
    gate = lax.conv_general_dilated(
        gate, conv_w[:, None, :], window_strides=(1,), padding=[(CONV_WIDTH - 1, 0)],
        dimension_numbers=('NWC', 'WIO', 'NWC'), feature_group_count=D_FF) + conv_b
    return (jax.nn.silu(gate) * val) @ w_down


def setup_inputs(seed: int = 0) -> dict:
    key = jax.random.key(seed)
    ks = jax.random.split(key, 32)
    f32 = jnp.float32
    nrm = lambda k, shape, s: jax.random.normal(k, shape, f32) * s
    L = DEPTH
    return {
        "x": nrm(ks[0], (BATCH, SEQ, D_MODEL), 1.0),
        "meta_tokens": nrm(ks[1], (N_META, D_MODEL), 1.0),
        "emb_ln_g": 1.0 + nrm(ks[2], (D_MODEL,), 0.02),
        "emb_ln_b": nrm(ks[3], (D_MODEL,), 0.02),
        "w_in": nrm(ks[4], (L, D_MODEL, IN_COLS), D_MODEL ** -0.5),
        "rwkv_mu": jax.random.uniform(ks[5], (L, RWKV_COLS), f32, 0.0, 1.0),
        "rwkv_w0": jax.random.uniform(ks[6], (L, RWKV_WIDTH), f32, -6.0, -1.0),
        "rwkv_w2": nrm(ks[7], (L, DECAY_LORA, RWKV_WIDTH), 0.1 * DECAY_LORA ** -0.5),
        "rwkv_a0": nrm(ks[8], (L, RWKV_WIDTH), 0.5),
        "rwkv_a2": nrm(ks[9], (L, ICLR_LORA, RWKV_WIDTH), ICLR_LORA ** -0.5),
        "rwkv_g2": nrm(ks[10], (L, GATE_LORA, RWKV_WIDTH), GATE_LORA ** -0.5),
        "rwkv_k_k": 0.85 + nrm(ks[11], (L, RWKV_WIDTH), 0.02),
        "rwkv_k_a": 1.0 + nrm(ks[12], (L, RWKV_WIDTH), 0.02),
        "rwkv_r_k": nrm(ks[13], (L, RWKV_HEADS, HEAD_DIM), 0.1),
        "rwkv_gn_g": 1.0 + nrm(ks[14], (L, RWKV_WIDTH), 0.02),
        "rwkv_gn_b": nrm(ks[15], (L, RWKV_WIDTH), 0.02),
        "sb_norm_g": 1.0 + nrm(ks[16], (L, SB_WIDTH), 0.02),
        "w_out": nrm(ks[17], (L, MIX_WIDTH, D_MODEL), MIX_WIDTH ** -0.5 * DEEPNORM_BETA),
        "ln1_g": 1.0 + nrm(ks[18], (L, D_MODEL), 0.02),
        "ln1_b": nrm(ks[19], (L, D_MODEL), 0.02),
        "ffn_w_up": nrm(ks[20], (L, D_MODEL, 2 * D_FF), D_MODEL ** -0.5),
        "ffn_conv_w": nrm(ks[21], (L, CONV_WIDTH, D_FF), CONV_WIDTH ** -0.5),
        "ffn_conv_b": nrm(ks[22], (L, D_FF), 0.02),
        "ffn_w_down": nrm(ks[23], (L, D_FF, D_MODEL), D_FF ** -0.5 * DEEPNORM_BETA),
        "ln2_g": 1.0 + nrm(ks[24], (L, D_MODEL), 0.02),
        "ln2_b": nrm(ks[25], (L, D_MODEL), 0.02),
    }


def reference(x, meta_tokens, emb_ln_g, emb_ln_b, w_in, rwkv_mu, rwkv_w0, rwkv_w2, rwkv_a0,
              rwkv_a2, rwkv_g2, rwkv_k_k, rwkv_k_a, rwkv_r_k, rwkv_gn_g, rwkv_gn_b, sb_norm_g,
              w_out, ln1_g, ln1_b, ffn_w_up, ffn_conv_w, ffn_conv_b, ffn_w_down, ln2_g, ln2_b):
    B = x.shape[0]
    meta = jnp.broadcast_to(meta_tokens[None].astype(x.dtype), (B, N_META, x.shape[-1]))
    h = layer_norm(jnp.concatenate([meta, x], axis=1), emb_ln_g, emb_ln_b)
    for l in range(DEPTH):
        p = h @ w_in[l]
        y_rwkv = rwkv7_mixer(p[..., :RWKV_COLS], rwkv_mu[l], rwkv_w0[l], rwkv_w2[l], rwkv_a0[l],
                             rwkv_a2[l], rwkv_g2[l], rwkv_k_k[l], rwkv_k_a[l], rwkv_r_k[l],
                             rwkv_gn_g[l], rwkv_gn_b[l])
        y_sb = stick_breaking_mixer(p[..., RWKV_COLS:], sb_norm_g[l])
        mix = jnp.concatenate([y_rwkv, y_sb], axis=-1) @ w_out[l]
        h = layer_norm(DEEPNORM_ALPHA * h + mix, ln1_g[l], ln1_b[l])
        ffn = conv_ffn(h, ffn_w_up[l], ffn_conv_w[l], ffn_conv_b[l], ffn_w_down[l])
        h = layer_norm(DEEPNORM_ALPHA * h + ffn, ln2_g[l], ln2_b[l])
    return h[:, N_META:]
```

```python
import functools

import jax
import jax.numpy as jnp
from jax import lax
from jax.experimental import pallas as pl
from jax.experimental.pallas import tpu as pltpu

F32 = jnp.float32
BF16 = jnp.bfloat16

D_MODEL = 2048
N_META = 16
HEAD_DIM = 64
RWKV_WIDTH = 1024
SB_WIDTH = 1024
DECAY_LORA = 64
ICLR_LORA = 64
GATE_LORA = 160
D_FF = 5632
DEPTH = 1
DEEPNORM_ALPHA = (2.0 * DEPTH) ** 0.25
LN_EPS = 1e-5
GN_EPS = 64e-5
RMS_EPS = 1e-6

LANES = 128
MXU_DIM = 256

GROUP = MXU_DIM
N_GROUPS = RWKV_WIDTH // GROUP
HEADS_PER_GROUP = GROUP // HEAD_DIM
CHUNK = 64
LOG2_CHUNK = 6
LOG2_HEAD = 6
LORA_W = 128
LORA_G = 256
LORA_COLS = 2 * LORA_W + LORA_G
RWKV_PCOLS = 3 * RWKV_WIDTH + LORA_COLS
SB_PCOLS = 3 * SB_WIDTH
SB_BLOCK = 128
META_PAD = 128

VMEM_LIMIT = 56 * 1024 * 1024


def _cparams(sem):
    return pltpu.CompilerParams(dimension_semantics=sem, vmem_limit_bytes=VMEM_LIMIT)


def _layer_norm(x, g, b):
    mu = jnp.mean(x, axis=-1, keepdims=True)
    xc = x - mu
    var = jnp.mean(xc * xc, axis=-1, keepdims=True)
    return xc * lax.rsqrt(var + LN_EPS) * g + b


def _dot(a, b):
    return jnp.dot(a, b, preferred_element_type=F32)


def _dot_nt(a, b):
    return lax.dot_general(a, b, (((1,), (1,)), ((), ())), preferred_element_type=F32)


def _split(x):
    hi = x.astype(BF16)
    lo = (x - hi.astype(F32)).astype(BF16)
    return hi, lo


IN_TN = 512
IN_NR = RWKV_PCOLS // IN_TN
IN_NS = SB_PCOLS // IN_TN


def _ln_inproj_kernel(x_ref, g_ref, b_ref, w_ref, h_ref, pr_ref, ps_ref, hb_ref):
    j = pl.program_id(1)

    @pl.when(j == 0)
    def _():
        h = _layer_norm(x_ref[...], g_ref[...], b_ref[...])
        h_ref[...] = h
        hb_ref[...] = h.astype(BF16)

    acc = _dot(hb_ref[...], w_ref[...])

    @pl.when(j < IN_NR)
    def _():
        pr_ref[...] = acc

    @pl.when(j >= IN_NR)
    def _():
        ps_ref[...] = acc.astype(BF16)


def _ln_inproj(x, g, b, w, tm):
    n = x.shape[0]
    return pl.pallas_call(
        _ln_inproj_kernel,
        grid=(n // tm, IN_NR + IN_NS),
        in_specs=[
            pl.BlockSpec((tm, D_MODEL), lambda i, j: (i, 0)),
            pl.BlockSpec((1, D_MODEL), lambda i, j: (0, 0)),
            pl.BlockSpec((1, D_MODEL), lambda i, j: (0, 0)),
            pl.BlockSpec((D_MODEL, IN_TN), lambda i, j: (0, j)),
        ],
        out_specs=[
            pl.BlockSpec((tm, D_MODEL), lambda i, j: (i, 0)),
            pl.BlockSpec((tm, IN_TN), lambda i, j: (i, jnp.minimum(j, IN_NR - 1))),
            pl.BlockSpec((tm, IN_TN), lambda i, j: (i, jnp.maximum(j - IN_NR, 0))),
        ],
        out_shape=[
            jax.ShapeDtypeStruct((n, D_MODEL), F32),
            jax.ShapeDtypeStruct((n, RWKV_PCOLS), F32),
            jax.ShapeDtypeStruct((n, SB_PCOLS), BF16),
        ],
        scratch_shapes=[pltpu.VMEM((tm, D_MODEL), BF16)],
        compiler_params=_cparams(("arbitrary", "arbitrary")),
        name="ln_inproj",
    )(x, g, b, w)


def _sb_block(qh, hmask, kb, vb, mask, upper, carry, acc):
    new_carry = []
    for h in range(2):
        z = _dot_nt(qh[h], kb)
        l1p = jnp.log1p(jnp.exp(-jnp.abs(z)))
        log_beta = jnp.minimum(z, 0.0) - l1p
        log_keep = jnp.minimum(-z, 0.0) - l1p
        if mask is not None:
            log_keep = jnp.where(mask, log_keep, 0.0)
        hi, lo = _split(log_keep)
        after = _dot(hi, upper) + _dot(lo, upper)
        w = jnp.exp(log_beta + after + carry[h])
        if mask is not None:
            w = jnp.where(mask, w, 0.0)
        pv = _dot(w.astype(BF16), vb)
        acc = acc + jnp.where(hmask[h], pv, 0.0)
        new_carry.append(carry[h] + jnp.sum(log_keep, axis=-1, keepdims=True))
    return new_carry, acc


def _sb_prologue(q):
    blk = q.shape[0]
    lane = lax.broadcasted_iota(jnp.int32, (blk, LANES), 1)
    row = lax.broadcasted_iota(jnp.int32, (blk, blk), 0)
    col = lax.broadcasted_iota(jnp.int32, (blk, blk), 1)
    hmask = (lane < HEAD_DIM, lane >= HEAD_DIM)
    qh = tuple(jnp.where(m, q, jnp.zeros_like(q)) for m in hmask)
    upper = (row > col).astype(BF16)
    return qh, hmask, upper, row, col


def _sb_epilogue(acc, hmask, g):
    sq = acc * acc
    ms = [jnp.sum(jnp.where(m, sq, 0.0), axis=-1, keepdims=True) * (1.0 / HEAD_DIM) for m in hmask]
    ms = jnp.where(hmask[0], ms[0], ms[1])
    return acc * lax.rsqrt(ms + RMS_EPS) * g


def _sb_kernel(q_ref, k_ref, v_ref, km_ref, vm_ref, g_ref, o_ref):
    qi = pl.program_id(2)
    qh, hmask, upper, row, col = _sb_prologue(q_ref[...])
    zero = jnp.zeros((SB_BLOCK, 1), F32)
    acc = jnp.zeros((SB_BLOCK, LANES), F32)

    start = pl.multiple_of(qi * SB_BLOCK, SB_BLOCK)
    carry, acc = _sb_block(qh, hmask, k_ref[pl.ds(start, SB_BLOCK), :], v_ref[pl.ds(start, SB_BLOCK), :],
                           col < row, upper, [zero, zero], acc)

    def body(n, state):
        c0, c1, a = state
        s = pl.multiple_of((qi - 1 - n) * SB_BLOCK, SB_BLOCK)
        (c0, c1), a = _sb_block(qh, hmask, k_ref[pl.ds(s, SB_BLOCK), :], v_ref[pl.ds(s, SB_BLOCK), :],
                                None, upper, [c0, c1], a)
        return c0, c1, a

    c0, c1, acc = lax.fori_loop(0, qi, body, (carry[0], carry[1], acc))
    _, acc = _sb_block(qh, hmask, km_ref[...], vm_ref[...], col < N_META, upper, [c0, c1], acc)
    o_ref[...] = _sb_epilogue(acc, hmask, g_ref[...]).astype(o_ref.dtype)


def _sb_meta_kernel(q_ref, k_ref, v_ref, g_ref, o_ref):
    qh, hmask, upper, row, col = _sb_prologue(q_ref[...])
    zero = jnp.zeros((META_PAD, 1), F32)
    acc = jnp.zeros((META_PAD, LANES), F32)
    _, acc = _sb_block(qh, hmask, k_ref[...], v_ref[...], col < row, upper, [zero, zero], acc)
    o_ref[...] = _sb_epilogue(acc, hmask, g_ref[...]).astype(o_ref.dtype)


def _sb_attention(ps, ps_meta, norm_g, batch, seq):
    n_hp = SB_WIDTH // LANES
    nq = seq // SB_BLOCK
    return pl.pallas_call(
        _sb_kernel,
        grid=(batch, n_hp, nq),
        in_specs=[
            pl.BlockSpec((SB_BLOCK, LANES), lambda b, h, q: (b * nq + q, h)),
            pl.BlockSpec((seq, LANES), lambda b, h, q: (b, n_hp + h)),
            pl.BlockSpec((seq, LANES), lambda b, h, q: (b, 2 * n_hp + h)),
            pl.BlockSpec((META_PAD, LANES), lambda b, h, q: (0, n_hp + h)),
            pl.BlockSpec((META_PAD, LANES), lambda b, h, q: (0, 2 * n_hp + h)),
            pl.BlockSpec((1, LANES), lambda b, h, q: (0, h)),
        ],
        out_specs=pl.BlockSpec((SB_BLOCK, LANES), lambda b, h, q: (b * nq + q, h)),
        out_shape=jax.ShapeDtypeStruct((batch * seq, SB_WIDTH), BF16),
        compiler_params=_cparams(("arbitrary", "arbitrary", "arbitrary")),
        name="sb_attention",
    )(ps, ps, ps, ps_meta, ps_meta, norm_g)


def _sb_attention_meta(ps_meta, norm_g):
    n_hp = SB_WIDTH // LANES
    return pl.pallas_call(
        _sb_meta_kernel,
        grid=(n_hp,),
        in_specs=[
            pl.BlockSpec((META_PAD, LANES), lambda h: (0, h)),
            pl.BlockSpec((META_PAD, LANES), lambda h: (0, n_hp + h)),
            pl.BlockSpec((META_PAD, LANES), lambda h: (0, 2 * n_hp + h)),
            pl.BlockSpec((1, LANES), lambda h: (0, h)),
        ],
        out_specs=pl.BlockSpec((META_PAD, LANES), lambda h: (0, h)),
        out_shape=jax.ShapeDtypeStruct((META_PAD, SB_WIDTH), BF16),
        compiler_params=_cparams(("arbitrary",)),
        name="sb_attention_meta",
    )(ps_meta, ps_meta, ps_meta, norm_g)


def _bd(x, mask):
    t = jnp.concatenate([x] * HEADS_PER_GROUP, axis=0)
    return jnp.where(mask, t, jnp.zeros_like(t))


def _mm_hi(a, b_bf16):
    hi, lo = _split(a)
    return _dot(hi, b_bf16) + _dot(lo, b_bf16)


def _mm3(a, b):
    ah, al = _split(a)
    bh, bl = _split(b)
    return _dot(ah, bh) + (_dot(al, bh) + _dot(ah, bl))


def _rwkv_kernel(pm_ref, pl_ref, prev_m_ref, prev_l_ref, s0_ref, mu_m_ref, mu_l_ref,
                 w0_ref, w2_ref, a0_ref, a2_ref, g2_ref, kk_ref, ka_ref, rk_ref, gng_ref, gnb_ref,
                 y_ref, s_out_ref, s_ref, carry_m_ref, carry_l_ref, *, n_valid):
    c = pl.program_id(2)
    L = CHUNK

    @pl.when(c == 0)
    def _():
        s_ref[...] = s0_ref[0]
        carry_m_ref[...] = prev_m_ref[...]
        carry_l_ref[...] = prev_l_ref[...]

    row = lax.broadcasted_iota(jnp.int32, (L, 1), 0)
    first = row == 0

    def token_shift(p, carry_ref, mu):
        prev_last = carry_ref[7:8, :]
        shifted = jnp.where(first, prev_last, pltpu.roll(p, 1, 0))
        carry_ref[...] = p[L - 8:, :]
        return p + (shifted - p) * mu

    xm = token_shift(pm_ref[...], carry_m_ref, mu_m_ref[0])
    xl = token_shift(pl_ref[...], carry_l_ref, mu_l_ref[...])
    r = xm[:, 0:GROUP]
    k = xm[:, GROUP:2 * GROUP]
    v = xm[:, 2 * GROUP:3 * GROUP]
    wl = xl[:, 0:LORA_W]
    al = xl[:, LORA_W:2 * LORA_W]
    gl = xl[:, 2 * LORA_W:]

    dw = w0_ref[0] + _dot(jnp.tanh(wl).astype(BF16), w2_ref[...])
    w_log = -(jnp.maximum(-dw, 0.0) + jnp.log1p(jnp.exp(-jnp.abs(dw)))) - 0.5
    logw = -jnp.exp(w_log)
    iclr = jax.nn.sigmoid(a0_ref[0] + _dot(al.astype(BF16), a2_ref[...]))
    gate = _dot(jax.nn.sigmoid(gl).astype(BF16), g2_ref[...])

    r4 = lax.broadcasted_iota(jnp.int32, (4 * L, GROUP), 0)
    c4 = lax.broadcasted_iota(jnp.int32, (4 * L, GROUP), 1)
    bd_mask = (r4 >> LOG2_CHUNK) == (c4 >> LOG2_HEAD)
    seg_ones = bd_mask.astype(BF16)
    ri = lax.broadcasted_iota(jnp.int32, (L, L), 0)
    ci = lax.broadcasted_iota(jnp.int32, (L, L), 1)
    tri_incl = (ci <= ri).astype(BF16)
    r2 = lax.broadcasted_iota(jnp.int32, (2 * L, 4 * L), 0)
    c2 = lax.broadcasted_iota(jnp.int32, (2 * L, 4 * L), 1) & (L - 1)
    low_mask = c2 < (r2 & (L - 1)) + (r2 >> LOG2_CHUNK)
    rl = lax.broadcasted_iota(jnp.int32, (L, 4 * L), 0)
    cl = lax.broadcasted_iota(jnp.int32, (L, 4 * L), 1) & (L - 1)
    eye_t = (rl == cl).astype(F32)

    kk = k * kk_ref[0]
    nrm = jnp.sqrt(_mm_hi(kk * kk, seg_ones))
    kk = kk / jnp.maximum(nrm, 1e-12)
    kmod = k * (1.0 + (iclr - 1.0) * ka_ref[0])

    if n_valid < L:
        valid = row < n_valid
        kk = jnp.where(valid, kk, 0.0)
        kmod = jnp.where(valid, kmod, 0.0)
        v = jnp.where(valid, v, 0.0)
        logw = jnp.where(valid, logw, 0.0)

    cum = _mm_hi_lhs_const(tri_incl, logw)
    p_in = jnp.exp(cum)
    p_ex = jnp.exp(cum - logw)
    p_inv = jnp.exp(-cum)
    p_tail = jnp.exp(cum[L - 1:L, :] - cum)
    at = -kk * p_ex
    kb = kk * iclr
    bt = kb * p_inv
    kt = kmod * p_inv
    rt = r * p_in

    ar = jnp.concatenate([at, rt], axis=0).astype(BF16)
    abr = jnp.where(low_mask, _dot_nt(ar, _bd(bt.astype(BF16), bd_mask)), 0.0)
    akr = jnp.where(low_mask, _dot_nt(ar, _bd(kt.astype(BF16), bd_mask)), 0.0)
    a_ab, r_b = abr[:L], abr[L:]
    a_ak, r_k = akr[:L], akr[L:]

    s_t = s_ref[...]
    sa = _dot_nt(ar, s_t.astype(BF16))
    a_s, r_s = sa[:L], sa[L:]
    vb = v.astype(BF16)
    bd_v = _bd(vb, bd_mask)
    x = a_s + _dot(a_ak.astype(BF16), bd_v)

    acc = eye_t
    pw = a_ab
    for lvl in range(LOG2_CHUNK):
        if lvl < LOG2_CHUNK - 1:
            res = _mm3(jnp.concatenate([acc, pw], axis=0), _bd(pw, bd_mask))
            acc = acc + res[:L]
            pw = res[L:]
        else:
            acc = acc + _mm3(acc, _bd(pw, bd_mask))
    u = _mm3(acc, _bd(x, bd_mask))

    ub = u.astype(BF16)
    rbk = jnp.concatenate([r_b, r_k], axis=1).astype(BF16)
    uv_bd = jnp.concatenate([_bd(ub, bd_mask), bd_v], axis=0)
    y = r_s + _dot(rbk, uv_bd)

    uv_t = jnp.transpose(jnp.concatenate([u, v], axis=0)).astype(BF16)
    bk_tail = jnp.concatenate([kb * p_tail, kmod * p_tail], axis=0).astype(BF16)
    s_new = s_t * jnp.exp(cum[L - 1:L, :]) + jnp.where(bd_mask, _dot(uv_t, bk_tail), 0.0)
    s_ref[...] = s_new
    s_out_ref[0] = s_new

    mean = _mm_hi(y, seg_ones) * (1.0 / HEAD_DIM)
    yc = y - mean
    var = _mm_hi(yc * yc, seg_ones) * (1.0 / HEAD_DIM)
    yn = yc * lax.rsqrt(var + GN_EPS) * gng_ref[0] + gnb_ref[0]
    bonus = _mm_hi(r * kmod * rk_ref[0], seg_ones) * v
    y_ref[...] = ((yn + bonus) * gate).astype(y_ref.dtype)


def _mm_hi_lhs_const(a_bf16, b):
    hi, lo = _split(b)
    return _dot(a_bf16, hi) + _dot(a_bf16, lo)


def _rwkv(p, prev, prev_blk, s0, prm, batch, n_chunks, n_valid):
    n_main = 3 * GROUP
    lora_blk = 3 * RWKV_WIDTH // LORA_COLS
    row_map = lambda b, g, c: (b * n_chunks + c, g)
    grp = lambda b, g, c: (g, 0, 0)
    grp2 = lambda b, g, c: (0, g)
    kern = functools.partial(_rwkv_kernel, n_valid=n_valid)
    return pl.pallas_call(
        kern,
        grid=(batch, N_GROUPS, n_chunks),
        in_specs=[
            pl.BlockSpec((CHUNK, n_main), row_map),
            pl.BlockSpec((CHUNK, LORA_COLS), lambda b, g, c: (b * n_chunks + c, lora_blk)),
            pl.BlockSpec((8, n_main), lambda b, g, c: (prev_blk, g)),
            pl.BlockSpec((8, LORA_COLS), lambda b, g, c: (prev_blk, lora_blk)),
            pl.BlockSpec((1, GROUP, GROUP), grp),
            pl.BlockSpec((1, 1, n_main), grp),
            pl.BlockSpec((1, LORA_COLS), lambda b, g, c: (0, 0)),
            pl.BlockSpec((1, 1, GROUP), grp),
            pl.BlockSpec((LORA_W, GROUP), grp2),
            pl.BlockSpec((1, 1, GROUP), grp),
            pl.BlockSpec((LORA_W, GROUP), grp2),
            pl.BlockSpec((LORA_G, GROUP), grp2),
            pl.BlockSpec((1, 1, GROUP), grp),
            pl.BlockSpec((1, 1, GROUP), grp),
            pl.BlockSpec((1, 1, GROUP), grp),
            pl.BlockSpec((1, 1, GROUP), grp),
            pl.BlockSpec((1, 1, GROUP), grp),
        ],
        out_specs=[
            pl.BlockSpec((CHUNK, GROUP), row_map),
            pl.BlockSpec((1, GROUP, GROUP), lambda b, g, c: (b * N_GROUPS + g, 0, 0)),
        ],
        out_shape=[
            jax.ShapeDtypeStruct((batch * n_chunks * CHUNK, RWKV_WIDTH), BF16),
            jax.ShapeDtypeStruct((batch * N_GROUPS, GROUP, GROUP), F32),
        ],
        scratch_shapes=[
            pltpu.VMEM((GROUP, GROUP), F32),
            pltpu.VMEM((8, n_main), F32),
            pltpu.VMEM((8, LORA_COLS), F32),
        ],
        compiler_params=_cparams(("arbitrary", "arbitrary", "arbitrary")),
        name="rwkv7",
    )(p, p, prev, prev, s0, prm["mu_m"], prm["mu_l"], prm["w0"], prm["w2"], prm["a0"], prm["a2"],
      prm["g2"], prm["k_k"], prm["k_a"], prm["r_k"], prm["gn_g"], prm["gn_b"])


def _outproj_ln_kernel(yr_ref, ys_ref, h_ref, w_ref, g_ref, b_ref, o_ref):
    mix = _dot(yr_ref[...], w_ref[0:RWKV_WIDTH, :]) + _dot(ys_ref[...], w_ref[RWKV_WIDTH:, :])
    o_ref[...] = _layer_norm(DEEPNORM_ALPHA * h_ref[...] + mix, g_ref[...], b_ref[...])


def _outproj_ln(yr, ys, h, w, g, b, tm):
    n = h.shape[0]
    return pl.pallas_call(
        _outproj_ln_kernel,
        grid=(n // tm,),
        in_specs=[
            pl.BlockSpec((tm, RWKV_WIDTH), lambda i: (i, 0)),
            pl.BlockSpec((tm, SB_WIDTH), lambda i: (i, 0)),
            pl.BlockSpec((tm, D_MODEL), lambda i: (i, 0)),
            pl.BlockSpec((D_MODEL, D_MODEL), lambda i: (0, 0)),
            pl.BlockSpec((1, D_MODEL), lambda i: (0, 0)),
            pl.BlockSpec((1, D_MODEL), lambda i: (0, 0)),
        ],
        out_specs=pl.BlockSpec((tm, D_MODEL), lambda i: (i, 0)),
        out_shape=jax.ShapeDtypeStruct((n, D_MODEL), F32),
        compiler_params=_cparams(("arbitrary",)),
        name="outproj_ln",
    )(yr, ys, h, w, g, b)


FF_TF = 512
FF_NF = D_FF // FF_TF


def _gate_halo_kernel(h_ref, w_ref, o_ref):
    o_ref[0] = _dot(h_ref[...].astype(BF16), w_ref[...])


def _gate_halo(h_rows, w_up):
    return pl.pallas_call(
        _gate_halo_kernel,
        grid=(FF_NF,),
        in_specs=[
            pl.BlockSpec((8, D_MODEL), lambda f: (N_META // 8 - 1, 0)),
            pl.BlockSpec((D_MODEL, FF_TF), lambda f: (0, f)),
        ],
        out_specs=pl.BlockSpec((1, 8, FF_TF), lambda f: (f, 0, 0)),
        out_shape=jax.ShapeDtypeStruct((FF_NF, 8, FF_TF), F32),
        compiler_params=_cparams(("arbitrary",)),
        name="ffn_gate_halo",
    )(h_rows, w_up)


def _ffn_kernel(h_ref, halo0_ref, wg_ref, wv_ref, cw_ref, cb_ref, wd_ref, g_ref, b_ref, o_ref,
                hb_ref, acc_ref, halo_ref, *, tiles_per_seq):
    i = pl.program_id(0)
    f = pl.program_id(1)
    tm = h_ref.shape[0]

    @pl.when(f == 0)
    def _():
        hb_ref[...] = h_ref[...].astype(BF16)
        acc_ref[...] = jnp.zeros_like(acc_ref)

    hb = hb_ref[...]
    gate = _dot(hb, wg_ref[...])
    val = _dot(hb, wv_ref[...])

    @pl.when((i % tiles_per_seq) == 0)
    def _():
        halo_ref[f] = halo0_ref[0]

    halo = halo_ref[f]
    halo_ref[f] = gate[tm - 8:, :]
    row = lax.broadcasted_iota(jnp.int32, (tm, 1), 0)
    g1 = jnp.where(row == 0, halo[7:8, :], pltpu.roll(gate, 1, 0))
    g2 = jnp.where(row == 0, halo[6:7, :], jnp.where(row == 1, halo[7:8, :], pltpu.roll(gate, 2, 0)))
    cw = cw_ref[...]
    conv = cw[0:1, :] * g2 + cw[1:2, :] * g1 + cw[2:3, :] * gate + cb_ref[...]
    act = (conv * jax.nn.sigmoid(conv) * val).astype(BF16)
    acc_ref[...] += _dot(act, wd_ref[...])

    @pl.when(f == FF_NF - 1)
    def _():
        o_ref[...] = _layer_norm(DEEPNORM_ALPHA * h_ref[...] + acc_ref[...], g_ref[...], b_ref[...])


def _ffn(h, halo0, w_up, conv_w, conv_b, w_down, g, b, tm, seq):
    n = h.shape[0]
    kern = functools.partial(_ffn_kernel, tiles_per_seq=seq // tm)
    return pl.pallas_call(
        kern,
        grid=(n // tm, FF_NF),
        in_specs=[
            pl.BlockSpec((tm, D_MODEL), lambda i, f: (i, 0)),
            pl.BlockSpec((1, 8, FF_TF), lambda i, f: (f, 0, 0)),
            pl.BlockSpec((D_MODEL, FF_TF), lambda i, f: (0, f)),
            pl.BlockSpec((D_MODEL, FF_TF), lambda i, f: (0, FF_NF + f)),
            pl.BlockSpec((3, FF_TF), lambda i, f: (0, f)),
            pl.BlockSpec((1, FF_TF), lambda i, f: (0, f)),
            pl.BlockSpec((FF_TF, D_MODEL), lambda i, f: (f, 0)),
            pl.BlockSpec((1, D_MODEL), lambda i, f: (0, 0)),
            pl.BlockSpec((1, D_MODEL), lambda i, f: (0, 0)),
        ],
        out_specs=pl.BlockSpec((tm, D_MODEL), lambda i, f: (i, 0)),
        out_shape=jax.ShapeDtypeStruct((n, D_MODEL), F32),
        scratch_shapes=[
            pltpu.VMEM((tm, D_MODEL), BF16),
            pltpu.VMEM((tm, D_MODEL), F32),
            pltpu.VMEM((FF_NF, 8, FF_TF), F32),
        ],
        compiler_params=_cparams(("arbitrary", "arbitrary")),
        name="conv_ffn",
    )(h, halo0, w_up, w_up, conv_w, conv_b, w_down, g, b)


def _group_major(a):
    parts = [a[..., s * RWKV_WIDTH:(s + 1) * RWKV_WIDTH] for s in range(3)]
    out = []
    for gi in range(N_GROUPS):
        out += [p[..., gi * GROUP:(gi + 1) * GROUP] for p in parts]
    return jnp.concatenate(out, axis=-1)


def _pad_cols(a, width):
    return jnp.pad(a, [(0, 0)] * (a.ndim - 1) + [(0, width - a.shape[-1])])


def _pad_rows(a, rows):
    return jnp.pad(a, [(0, rows - a.shape[0])] + [(0, 0)] * (a.ndim - 1))


def _lora_slots(a):
    c = 3 * RWKV_WIDTH
    wl = a[..., c:c + DECAY_LORA]
    al = a[..., c + DECAY_LORA:c + DECAY_LORA + ICLR_LORA]
    gl = a[..., c + DECAY_LORA + ICLR_LORA:c + DECAY_LORA + ICLR_LORA + GATE_LORA]
    return jnp.concatenate([_pad_cols(wl, LORA_W), _pad_cols(al, LORA_W), _pad_cols(gl, LORA_G)], axis=-1)


def kernel(x, meta_tokens, emb_ln_g, emb_ln_b, w_in, rwkv_mu, rwkv_w0, rwkv_w2, rwkv_a0, rwkv_a2, rwkv_g2, rwkv_k_k, rwkv_k_a, rwkv_r_k, rwkv_gn_g, rwkv_gn_b, sb_norm_g, w_out, ln1_g, ln1_b, ffn_w_up, ffn_conv_w, ffn_conv_b, ffn_w_down, ln2_g, ln2_b):
    batch, seq, d = x.shape
    assert d == D_MODEL and seq % 512 == 0 and w_in.shape[0] == 1
    n_rw = 3 * RWKV_WIDTH + DECAY_LORA + ICLR_LORA + GATE_LORA
    row2 = lambda a: a.reshape(1, -1)
    grp3 = lambda a: a.reshape(N_GROUPS, 1, GROUP)

    wi = w_in[0]
    sb_cols = wi[:, n_rw:]
    sb_cols = jnp.concatenate([sb_cols[:, :SB_WIDTH] * (HEAD_DIM ** -0.5), sb_cols[:, SB_WIDTH:]], axis=1)
    w1 = jnp.concatenate([_group_major(wi[:, :3 * RWKV_WIDTH]), _lora_slots(wi), sb_cols], axis=1).astype(BF16)
    mu = rwkv_mu[0]
    prm = {
        "mu_m": _group_major(mu[:3 * RWKV_WIDTH]).reshape(N_GROUPS, 1, 3 * GROUP),
        "mu_l": row2(_lora_slots(mu)),
        "w0": grp3(rwkv_w0[0]),
        "w2": _pad_rows(rwkv_w2[0], LORA_W).astype(BF16),
        "a0": grp3(rwkv_a0[0]),
        "a2": _pad_rows(rwkv_a2[0], LORA_W).astype(BF16),
        "g2": _pad_rows(rwkv_g2[0], LORA_G).astype(BF16),
        "k_k": grp3(rwkv_k_k[0]),
        "k_a": grp3(rwkv_k_a[0]),
        "r_k": grp3(rwkv_r_k[0].reshape(-1)),
        "gn_g": grp3(rwkv_gn_g[0]),
        "gn_b": grp3(rwkv_gn_b[0]),
    }
    w_o = w_out[0].astype(BF16)
    w_up = ffn_w_up[0].astype(BF16)
    w_dn = ffn_w_down[0].astype(BF16)
    g0, b0 = row2(emb_ln_g), row2(emb_ln_b)
    g1, b1 = row2(ln1_g[0]), row2(ln1_b[0])
    g2, b2 = row2(ln2_g[0]), row2(ln2_b[0])
    sbg = row2(sb_norm_g[0])

    xm = _pad_rows(meta_tokens.astype(x.dtype), META_PAD)
    h_m, pr_m, ps_m = _ln_inproj(xm, g0, b0, w1, META_PAD)
    ysb_m = _sb_attention_meta(ps_m, sbg)
    zeros_state = jnp.zeros((N_GROUPS, GROUP, GROUP), F32)
    yr_m, s_meta = _rwkv(pr_m, jnp.zeros((8, RWKV_PCOLS), F32), 0, zeros_state, prm, 1, 1, N_META)
    h1_m = _outproj_ln(_pad_rows(yr_m, META_PAD), ysb_m, h_m, w_o, g1, b1, META_PAD)
    halo0 = _gate_halo(h1_m, w_up)

    xr = x.reshape(batch * seq, d)
    h0, pr, ps = _ln_inproj(xr, g0, b0, w1, 512)
    y_sb = _sb_attention(ps, ps_m, sbg, batch, seq)
    y_rw, _ = _rwkv(pr, pr_m, N_META // 8 - 1, s_meta, prm, batch, seq // CHUNK, CHUNK)
    h1 = _outproj_ln(y_rw, y_sb, h0, w_o, g1, b1, 512)
    out = _ffn(h1, halo0, w_up, ffn_conv_w[0], row2(ffn_conv_b[0]), w_dn, g2, b2, 512, seq)
    return out.reshape(batch, seq, d)
```

```python
import functools

import jax
import jax.numpy as jnp
from jax import lax
from jax.experimental import pallas as pl
from jax.experimental.pallas import tpu as pltpu

F32 = jnp.float32
BF16 = jnp.bfloat16

D_MODEL = 2048
N_META = 16
HEAD_DIM = 64
RWKV_WIDTH = 1024
SB_WIDTH = 1024
DECAY_LORA = 64
ICLR_LORA = 64
GATE_LORA = 160
D_FF = 5632
DEPTH = 1
DEEPNORM_ALPHA = (2.0 * DEPTH) ** 0.25
LN_EPS = 1e-5
GN_EPS = 64e-5
RMS_EPS = 1e-6

LANES = 128
MXU_DIM = 256

GROUP = MXU_DIM
N_GROUPS = RWKV_WIDTH // GROUP
HEADS_PER_GROUP = GROUP // HEAD_DIM
CHUNK = 64
LOG2_CHUNK = 6
LOG2_HEAD = 6
LORA_W = 128
LORA_G = 256
LORA_COLS = 2 * LORA_W + LORA_G
RWKV_PCOLS = 3 * RWKV_WIDTH + LORA_COLS
SB_PCOLS = 3 * SB_WIDTH
SB_BLOCK = 128
META_PAD = 128

VMEM_LIMIT = 56 * 1024 * 1024


def _cparams(sem, flags=None):
    return pltpu.CompilerParams(dimension_semantics=sem, vmem_limit_bytes=VMEM_LIMIT, flags=flags)


def _layer_norm(x, g, b):
    mu = jnp.mean(x, axis=-1, keepdims=True)
    xc = x - mu
    var = jnp.mean(xc * xc, axis=-1, keepdims=True)
    return xc * lax.rsqrt(var + LN_EPS) * g + b


def _dot(a, b):
    return jnp.dot(a, b, preferred_element_type=F32)


def _dot_nt(a, b):
    return lax.dot_general(a, b, (((1,), (1,)), ((), ())), preferred_element_type=F32)


def _split(x):
    hi = x.astype(BF16)
    lo = (x - hi.astype(F32)).astype(BF16)
    return hi, lo


def _each(f, *lists):
    return [f(*args) for args in zip(*lists)]


IN_TN = 512
IN_NR = RWKV_PCOLS // IN_TN
IN_NS = SB_PCOLS // IN_TN


def _ln_inproj_kernel(x_ref, g_ref, b_ref, w_ref, h_ref, pr_ref, ps_ref, hb_ref):
    j = pl.program_id(1)

    @pl.when(j == 0)
    def _():
        h = _layer_norm(x_ref[...], g_ref[...], b_ref[...])
        h_ref[...] = h
        hb_ref[...] = h.astype(BF16)

    acc = _dot(hb_ref[...], w_ref[...])

    @pl.when(j < IN_NR)
    def _():
        pr_ref[...] = acc

    @pl.when(j >= IN_NR)
    def _():
        ps_ref[...] = acc.astype(BF16)


def _ln_inproj(x, g, b, w, tm):
    n = x.shape[0]
    return pl.pallas_call(
        _ln_inproj_kernel,
        grid=(n // tm, IN_NR + IN_NS),
        in_specs=[
            pl.BlockSpec((tm, D_MODEL), lambda i, j: (i, 0)),
            pl.BlockSpec((1, D_MODEL), lambda i, j: (0, 0)),
            pl.BlockSpec((1, D_MODEL), lambda i, j: (0, 0)),
            pl.BlockSpec((D_MODEL, IN_TN), lambda i, j: (0, j)),
        ],
        out_specs=[
            pl.BlockSpec((tm, D_MODEL), lambda i, j: (i, 0)),
            pl.BlockSpec((tm, IN_TN), lambda i, j: (i, jnp.minimum(j, IN_NR - 1))),
            pl.BlockSpec((tm, IN_TN), lambda i, j: (i, jnp.maximum(j - IN_NR, 0))),
        ],
        out_shape=[
            jax.ShapeDtypeStruct((n, D_MODEL), F32),
            jax.ShapeDtypeStruct((n, RWKV_PCOLS), F32),
            jax.ShapeDtypeStruct((n, SB_PCOLS), BF16),
        ],
        scratch_shapes=[pltpu.VMEM((tm, D_MODEL), BF16)],
        compiler_params=_cparams(("arbitrary", "arbitrary")),
        name="ln_inproj",
    )(x, g, b, w)


SB_TQ = 256
SB_W = MXU_DIM
SB_HEADS = SB_W // HEAD_DIM


def _sb_setup(carry_ref, acc_ref, tq):
    r = lax.broadcasted_iota(jnp.int32, (2 * SB_BLOCK, 2 * SB_BLOCK), 0) & (SB_BLOCK - 1)
    c = lax.broadcasted_iota(jnp.int32, (2 * SB_BLOCK, 2 * SB_BLOCK), 1)
    u = ((c >= SB_BLOCK) | (r > c)).astype(BF16)
    rb = lax.broadcasted_iota(jnp.int32, (SB_HEADS * SB_BLOCK, SB_W), 0)
    cb = lax.broadcasted_iota(jnp.int32, (SB_HEADS * SB_BLOCK, SB_W), 1)
    bd_mask = (rb >> 7) == (cb >> LOG2_HEAD)
    carry_ref[...] = jnp.zeros_like(carry_ref)
    acc_ref[...] = jnp.zeros_like(acc_ref)
    row = lax.broadcasted_iota(jnp.int32, (tq, SB_BLOCK), 0)
    col = lax.broadcasted_iota(jnp.int32, (tq, SB_BLOCK), 1)
    return u, bd_mask, row, col


def _sb_step(q, kb, vb, mask, u, bd_mask, carry_ref, acc_ref):
    kbd = jnp.where(bd_mask, jnp.concatenate([kb] * SB_HEADS, axis=0), jnp.zeros((), BF16))
    vbd = jnp.where(bd_mask, jnp.concatenate([vb] * SB_HEADS, axis=0), jnp.zeros((), BF16))
    z_all = _dot_nt(q, kbd)
    z = [z_all[:, h * SB_BLOCK:(h + 1) * SB_BLOCK] for h in range(SB_HEADS)]
    t = _each(lambda a: jnp.log(1.0 + jnp.exp(-jnp.abs(a))), z)
    log_beta = _each(lambda a, b: jnp.minimum(a, 0.0) - b, z, t)
    log_keep = _each(lambda a, b: a - b, log_beta, z)
    if mask is not None:
        log_keep = _each(lambda a: jnp.where(mask, a, 0.0), log_keep)
    hl = _each(lambda a: jnp.concatenate(_split(a), axis=1), log_keep)
    at = _each(lambda a: _dot(a, u), hl)
    carry = [carry_ref[h] for h in range(SB_HEADS)]
    w = _each(lambda a, b, c: jnp.exp(a + b[:, :SB_BLOCK] + c), log_beta, at, carry)
    if mask is not None:
        w = _each(lambda a: jnp.where(mask, a, 0.0), w)
    for h in range(SB_HEADS):
        carry_ref[h] = carry[h] + at[h][:, SB_BLOCK:]
    acc_ref[...] += _dot(jnp.concatenate([a.astype(BF16) for a in w], axis=1), vbd)


def _sb_finish(acc_ref, g_ref, o_ref):
    tq = acc_ref.shape[0]
    lane = lax.broadcasted_iota(jnp.int32, (tq, LANES), 1)
    hm = (lane < HEAD_DIM, lane >= HEAD_DIM)
    for hp in range(SB_W // LANES):
        sl = slice(hp * LANES, (hp + 1) * LANES)
        acc = acc_ref[:, sl]
        sq = acc * acc
        ms = [jnp.sum(jnp.where(m, sq, 0.0), axis=-1, keepdims=True) * (1.0 / HEAD_DIM) for m in hm]
        ms = jnp.where(hm[0], ms[0], ms[1])
        o_ref[:, sl] = (acc * lax.rsqrt(ms + RMS_EPS) * g_ref[:, sl]).astype(o_ref.dtype)


def _sb_kernel(q_ref, k_ref, v_ref, km_ref, vm_ref, g_ref, o_ref, carry_ref, acc_ref, *, tq):
    qi = pl.program_id(2)
    nkb = tq // SB_BLOCK
    u, bd_mask, row, col = _sb_setup(carry_ref, acc_ref, tq)
    q = q_ref[...]

    def visit(j, mask):
        s = pl.multiple_of(j * SB_BLOCK, SB_BLOCK)
        _sb_step(q, k_ref[pl.ds(s, SB_BLOCK), :], v_ref[pl.ds(s, SB_BLOCK), :], mask, u, bd_mask,
                 carry_ref, acc_ref)

    for d in range(nkb):
        visit((qi + 1) * nkb - 1 - d, col + (nkb - 1 - d) * SB_BLOCK < row)

    def body(n, carry):
        visit(qi * nkb - 1 - n, None)
        return carry

    lax.fori_loop(0, qi * nkb, body, 0)
    _sb_step(q, km_ref[...], vm_ref[...], col < N_META, u, bd_mask, carry_ref, acc_ref)
    _sb_finish(acc_ref, g_ref, o_ref)


def _sb_meta_kernel(q_ref, k_ref, v_ref, g_ref, o_ref, carry_ref, acc_ref):
    u, bd_mask, row, col = _sb_setup(carry_ref, acc_ref, META_PAD)
    _sb_step(q_ref[...], k_ref[...], v_ref[...], col < row, u, bd_mask, carry_ref, acc_ref)
    _sb_finish(acc_ref, g_ref, o_ref)


def _sb_scratch(tq):
    return [pltpu.VMEM((SB_HEADS, tq, SB_BLOCK), F32), pltpu.VMEM((tq, SB_W), F32)]


def _sb_attention(ps, ps_meta, norm_g, batch, seq):
    w = SB_W
    n_g = SB_WIDTH // w
    nq = seq // SB_TQ
    kern = functools.partial(_sb_kernel, tq=SB_TQ)
    return pl.pallas_call(
        kern,
        grid=(batch, n_g, nq),
        in_specs=[
            pl.BlockSpec((SB_TQ, w), lambda b, h, q: (b * nq + q, h)),
            pl.BlockSpec((seq, w), lambda b, h, q: (b, n_g + h)),
            pl.BlockSpec((seq, w), lambda b, h, q: (b, 2 * n_g + h)),
            pl.BlockSpec((META_PAD, w), lambda b, h, q: (0, n_g + h)),
            pl.BlockSpec((META_PAD, w), lambda b, h, q: (0, 2 * n_g + h)),
            pl.BlockSpec((1, w), lambda b, h, q: (0, h)),
        ],
        out_specs=pl.BlockSpec((SB_TQ, w), lambda b, h, q: (b * nq + q, h)),
        out_shape=jax.ShapeDtypeStruct((batch * seq, SB_WIDTH), BF16),
        scratch_shapes=_sb_scratch(SB_TQ),
        compiler_params=_cparams(("arbitrary", "arbitrary", "arbitrary")),
        name="sb_attention",
    )(ps, ps, ps, ps_meta, ps_meta, norm_g)


def _sb_attention_meta(ps_meta, norm_g):
    w = SB_W
    n_g = SB_WIDTH // w
    return pl.pallas_call(
        _sb_meta_kernel,
        grid=(n_g,),
        in_specs=[
            pl.BlockSpec((META_PAD, w), lambda h: (0, h)),
            pl.BlockSpec((META_PAD, w), lambda h: (0, n_g + h)),
            pl.BlockSpec((META_PAD, w), lambda h: (0, 2 * n_g + h)),
            pl.BlockSpec((1, w), lambda h: (0, h)),
        ],
        out_specs=pl.BlockSpec((META_PAD, w), lambda h: (0, h)),
        out_shape=jax.ShapeDtypeStruct((META_PAD, SB_WIDTH), BF16),
        scratch_shapes=_sb_scratch(META_PAD),
        compiler_params=_cparams(("arbitrary",)),
        name="sb_attention_meta",
    )(ps_meta, ps_meta, ps_meta, norm_g)


def _bd(x, mask):
    t = jnp.concatenate([x] * HEADS_PER_GROUP, axis=0)
    return jnp.where(mask, t, jnp.zeros_like(t))


def _mm_hi(a, b_bf16):
    hi, lo = _split(a)
    return _dot(hi, b_bf16) + _dot(lo, b_bf16)


def _mm3(a, b):
    ah, al = _split(a)
    bh, bl = _split(b)
    return _dot(ah, bh) + (_dot(al, bh) + _dot(ah, bl))


def _rwkv_kernel(pm_ref, pl_ref, prev_m_ref, prev_l_ref, s0_ref, mu_m_ref, mu_l_ref,
                 w0_ref, w2_ref, a0_ref, a2_ref, g2_ref, kk_ref, ka_ref, rk_ref, gng_ref, gnb_ref,
                 y_ref, s_out_ref, s_ref, carry_m_ref, carry_l_ref, *, n_valid):
    c = pl.program_id(1)
    L = CHUNK

    @pl.when(c == 0)
    def _():
        s_ref[...] = s0_ref[...]
        carry_m_ref[...] = prev_m_ref[...]
        carry_l_ref[...] = prev_l_ref[...]

    row = lax.broadcasted_iota(jnp.int32, (L, 1), 0)
    first = row == 0

    def token_shift(p, prev_last, mu):
        shifted = jnp.where(first, prev_last, pltpu.roll(p, 1, 0))
        return p + (shifted - p) * mu

    pl_blk = pl_ref[...]
    xl = token_shift(pl_blk, carry_l_ref[7:8, :], mu_l_ref[...])
    carry_l_ref[...] = pl_blk[L - 8:, :]
    lora = (jnp.tanh(xl[:, 0:LORA_W]).astype(BF16), xl[:, LORA_W:2 * LORA_W].astype(BF16),
            jax.nn.sigmoid(xl[:, 2 * LORA_W:]).astype(BF16))

    r4 = lax.broadcasted_iota(jnp.int32, (4 * L, GROUP), 0)
    c4 = lax.broadcasted_iota(jnp.int32, (4 * L, GROUP), 1)
    bd_mask = (r4 >> LOG2_CHUNK) == (c4 >> LOG2_HEAD)
    seg_ones = bd_mask.astype(BF16)
    ri = lax.broadcasted_iota(jnp.int32, (L, L), 0)
    ci = lax.broadcasted_iota(jnp.int32, (L, L), 1)
    tri_incl = (ci <= ri).astype(BF16)
    r2 = lax.broadcasted_iota(jnp.int32, (2 * L, 4 * L), 0)
    c2 = lax.broadcasted_iota(jnp.int32, (2 * L, 4 * L), 1) & (L - 1)
    low_mask = c2 < (r2 & (L - 1)) + (r2 >> LOG2_CHUNK)
    rl = lax.broadcasted_iota(jnp.int32, (L, 4 * L), 0)
    cl = lax.broadcasted_iota(jnp.int32, (L, 4 * L), 1) & (L - 1)
    eye_t = (rl == cl).astype(F32)
    consts = (row, first, bd_mask, seg_ones, tri_incl, low_mask, eye_t)

    pm_blk = pm_ref[...]
    xm_all = token_shift(pm_blk, carry_m_ref[7:8, :], mu_m_ref[...])
    carry_m_ref[...] = pm_blk[L - 8:, :]
    gsl = [slice(g * GROUP, (g + 1) * GROUP) for g in range(N_GROUPS)]
    xm = [xm_all[:, g * 3 * GROUP:(g + 1) * 3 * GROUP] for g in range(N_GROUPS)]
    prm = [[ref[:, gs] for gs in gsl] for ref in
           (w0_ref, w2_ref, a0_ref, a2_ref, g2_ref, kk_ref, ka_ref, rk_ref, gng_ref, gnb_ref)]
    ys, s_new = _rwkv_groups(xm, lora, prm, [s_ref[g] for g in range(N_GROUPS)], consts, n_valid)
    for g in range(N_GROUPS):
        s_ref[g] = s_new[g]
        s_out_ref[g] = s_new[g]
        y_ref[:, gsl[g]] = ys[g].astype(y_ref.dtype)


def _rwkv_groups(xm, lora, prm, s_t, consts, n_valid):
    L = CHUNK
    row, first, bd_mask, seg_ones, tri_incl, low_mask, eye_t = consts
    w0, w2, a0, a2, g2, k_k, k_a, r_k, gn_g, gn_b = prm
    tanh_wl, al, sig_gl = lora
    r = [x[:, 0:GROUP] for x in xm]
    k = [x[:, GROUP:2 * GROUP] for x in xm]
    v = [x[:, 2 * GROUP:3 * GROUP] for x in xm]

    dw = _each(lambda b, w: b + _dot(tanh_wl, w), w0, w2)
    w_log = _each(lambda d: -(jnp.maximum(-d, 0.0) + jnp.log1p(jnp.exp(-jnp.abs(d)))) - 0.5, dw)
    logw = _each(lambda x: -jnp.exp(x), w_log)
    iclr = _each(lambda b, w: jax.nn.sigmoid(b + _dot(al, w)), a0, a2)
    gate = _each(lambda w: _dot(sig_gl, w), g2)

    kk = _each(lambda a, b: a * b, k, k_k)
    nrm = _each(lambda x: jnp.sqrt(_mm_hi(x * x, seg_ones)), kk)
    kk = _each(lambda x, n: x / jnp.maximum(n, 1e-12), kk, nrm)
    kmod = _each(lambda x, i, a: x * (1.0 + (i - 1.0) * a), k, iclr, k_a)

    if n_valid < L:
        valid = row < n_valid
        zero_pad = lambda x: jnp.where(valid, x, 0.0)
        kk, kmod, v, logw = _each(zero_pad, kk), _each(zero_pad, kmod), _each(zero_pad, v), _each(zero_pad, logw)

    cum = _each(lambda x: _mm_hi_lhs_const(tri_incl, x), logw)
    p_in = _each(jnp.exp, cum)
    p_ex = _each(lambda c, w: jnp.exp(c - w), cum, logw)
    p_inv = _each(lambda c: jnp.exp(-c), cum)
    p_tail = _each(lambda c: jnp.exp(c[L - 1:L, :] - c), cum)
    p_last = _each(lambda c: jnp.exp(c[L - 1:L, :]), cum)
    at = _each(lambda a, p: -a * p, kk, p_ex)
    kb = _each(lambda a, i: a * i, kk, iclr)
    bt = _each(lambda a, p: (a * p).astype(BF16), kb, p_inv)
    kt = _each(lambda a, p: (a * p).astype(BF16), kmod, p_inv)
    rt = _each(lambda a, p: a * p, r, p_in)

    ar = _each(lambda a, b: jnp.concatenate([a, b], axis=0).astype(BF16), at, rt)
    abr = _each(lambda a, b: jnp.where(low_mask, _dot_nt(a, _bd(b, bd_mask)), 0.0), ar, bt)
    akr = _each(lambda a, b: jnp.where(low_mask, _dot_nt(a, _bd(b, bd_mask)), 0.0), ar, kt)
    sa = _each(lambda a, s: _dot_nt(a, s.astype(BF16)), ar, s_t)
    bd_v = _each(lambda x: _bd(x.astype(BF16), bd_mask), v)
    x = _each(lambda s, a, b: s[:L] + _dot(a[:L].astype(BF16), b), sa, akr, bd_v)

    acc = [eye_t] * N_GROUPS
    pw = [a[:L] for a in abr]
    for lvl in range(LOG2_CHUNK):
        if lvl < LOG2_CHUNK - 1:
            res = _each(lambda a, p: _mm3(jnp.concatenate([a, p], axis=0), _bd(p, bd_mask)), acc, pw)
            acc = _each(lambda a, rr: a + rr[:L], acc, res)
            pw = [rr[L:] for rr in res]
        else:
            acc = _each(lambda a, p: a + _mm3(a, _bd(p, bd_mask)), acc, pw)
    u = _each(lambda a, b: _mm3(a, _bd(b, bd_mask)), acc, x)

    rbk = _each(lambda a, b: jnp.concatenate([a[L:], b[L:]], axis=1).astype(BF16), abr, akr)
    uv_bd = _each(lambda a, b: jnp.concatenate([_bd(a.astype(BF16), bd_mask), b], axis=0), u, bd_v)
    y = _each(lambda s, a, b: s[L:] + _dot(a, b), sa, rbk, uv_bd)

    uv_t = _each(lambda a, b: jnp.transpose(jnp.concatenate([a, b], axis=0)).astype(BF16), u, v)
    bk_tail = _each(lambda a, b, p: jnp.concatenate([a * p, b * p], axis=0).astype(BF16), kb, kmod, p_tail)
    s_new = _each(lambda s, p, a, b: s * p + jnp.where(bd_mask, _dot(a, b), 0.0), s_t, p_last, uv_t, bk_tail)

    mean = _each(lambda a: _mm_hi(a, seg_ones) * (1.0 / HEAD_DIM), y)
    yc = _each(lambda a, m: a - m, y, mean)
    var = _each(lambda a: _mm_hi(a * a, seg_ones) * (1.0 / HEAD_DIM), yc)
    yn = _each(lambda a, s, g_, b_: a * lax.rsqrt(s + GN_EPS) * g_ + b_, yc, var, gn_g, gn_b)
    bonus = _each(lambda a, b, c, d: _mm_hi(a * b * c, seg_ones) * d, r, kmod, r_k, v)
    out = _each(lambda a, b, c: (a + b) * c, yn, bonus, gate)
    return out, s_new


def _mm_hi_lhs_const(a_bf16, b):
    hi, lo = _split(b)
    return _dot(a_bf16, hi) + _dot(a_bf16, lo)


def _rwkv(p, prev, prev_blk, s0, prm, batch, n_chunks, n_valid):
    n_main = 3 * RWKV_WIDTH
    lora_blk = n_main // LORA_COLS
    row_map = lambda b, c: (b * n_chunks + c, 0)
    whole = lambda b, c: (0, 0)
    vec = pl.BlockSpec((1, RWKV_WIDTH), whole)
    kern = functools.partial(_rwkv_kernel, n_valid=n_valid)
    return pl.pallas_call(
        kern,
        grid=(batch, n_chunks),
        in_specs=[
            pl.BlockSpec((CHUNK, n_main), row_map),
            pl.BlockSpec((CHUNK, LORA_COLS), lambda b, c: (b * n_chunks + c, lora_blk)),
            pl.BlockSpec((8, n_main), lambda b, c: (prev_blk, 0)),
            pl.BlockSpec((8, LORA_COLS), lambda b, c: (prev_blk, lora_blk)),
            pl.BlockSpec((N_GROUPS, GROUP, GROUP), lambda b, c: (0, 0, 0)),
            pl.BlockSpec((1, n_main), whole),
            pl.BlockSpec((1, LORA_COLS), whole),
            vec,
            pl.BlockSpec((LORA_W, RWKV_WIDTH), whole),
            vec,
            pl.BlockSpec((LORA_W, RWKV_WIDTH), whole),
            pl.BlockSpec((LORA_G, RWKV_WIDTH), whole),
            vec, vec, vec, vec, vec,
        ],
        out_specs=[
            pl.BlockSpec((CHUNK, RWKV_WIDTH), row_map),
            pl.BlockSpec((N_GROUPS, GROUP, GROUP), lambda b, c: (b, 0, 0)),
        ],
        out_shape=[
            jax.ShapeDtypeStruct((batch * n_chunks * CHUNK, RWKV_WIDTH), BF16),
            jax.ShapeDtypeStruct((batch * N_GROUPS, GROUP, GROUP), F32),
        ],
        scratch_shapes=[
            pltpu.VMEM((N_GROUPS, GROUP, GROUP), F32),
            pltpu.VMEM((8, n_main), F32),
            pltpu.VMEM((8, LORA_COLS), F32),
        ],
        compiler_params=_cparams(("arbitrary", "arbitrary")),
        name="rwkv7",
    )(p, p, prev, prev, s0, prm["mu_m"], prm["mu_l"], prm["w0"], prm["w2"], prm["a0"], prm["a2"],
      prm["g2"], prm["k_k"], prm["k_a"], prm["r_k"], prm["gn_g"], prm["gn_b"])


def _outproj_ln_kernel(yr_ref, ys_ref, h_ref, w_ref, g_ref, b_ref, o_ref):
    mix = _dot(yr_ref[...], w_ref[0:RWKV_WIDTH, :]) + _dot(ys_ref[...], w_ref[RWKV_WIDTH:, :])
    o_ref[...] = _layer_norm(DEEPNORM_ALPHA * h_ref[...] + mix, g_ref[...], b_ref[...])


def _outproj_ln(yr, ys, h, w, g, b, tm):
    n = h.shape[0]
    return pl.pallas_call(
        _outproj_ln_kernel,
        grid=(n // tm,),
        in_specs=[
            pl.BlockSpec((tm, RWKV_WIDTH), lambda i: (i, 0)),
            pl.BlockSpec((tm, SB_WIDTH), lambda i: (i, 0)),
            pl.BlockSpec((tm, D_MODEL), lambda i: (i, 0)),
            pl.BlockSpec((D_MODEL, D_MODEL), lambda i: (0, 0)),
            pl.BlockSpec((1, D_MODEL), lambda i: (0, 0)),
            pl.BlockSpec((1, D_MODEL), lambda i: (0, 0)),
        ],
        out_specs=pl.BlockSpec((tm, D_MODEL), lambda i: (i, 0)),
        out_shape=jax.ShapeDtypeStruct((n, D_MODEL), F32),
        compiler_params=_cparams(("arbitrary",)),
        name="outproj_ln",
    )(yr, ys, h, w, g, b)


FF_TF = 512
FF_NF = D_FF // FF_TF


def _gate_halo_kernel(h_ref, w_ref, o_ref):
    o_ref[0] = _dot(h_ref[...].astype(BF16), w_ref[...])


def _gate_halo(h_rows, w_up):
    return pl.pallas_call(
        _gate_halo_kernel,
        grid=(FF_NF,),
        in_specs=[
            pl.BlockSpec((8, D_MODEL), lambda f: (N_META // 8 - 1, 0)),
            pl.BlockSpec((D_MODEL, FF_TF), lambda f: (0, f)),
        ],
        out_specs=pl.BlockSpec((1, 8, FF_TF), lambda f: (f, 0, 0)),
        out_shape=jax.ShapeDtypeStruct((FF_NF, 8, FF_TF), F32),
        compiler_params=_cparams(("arbitrary",)),
        name="ffn_gate_halo",
    )(h_rows, w_up)


def _ffn_kernel(h_ref, halo0_ref, wg_ref, wv_ref, cw_ref, cb_ref, wd_ref, g_ref, b_ref, o_ref,
                hb_ref, acc_ref, halo_ref, *, tiles_per_seq):
    i = pl.program_id(0)
    f = pl.program_id(1)
    tm = h_ref.shape[0]

    @pl.when(f == 0)
    def _():
        hb_ref[...] = h_ref[...].astype(BF16)
        acc_ref[...] = jnp.zeros_like(acc_ref)

    hb = hb_ref[...]
    gate = _dot(hb, wg_ref[...])
    val = _dot(hb, wv_ref[...])

    @pl.when((i % tiles_per_seq) == 0)
    def _():
        halo_ref[f] = halo0_ref[0]

    halo = halo_ref[f]
    halo_ref[f] = gate[tm - 8:, :]
    row = lax.broadcasted_iota(jnp.int32, (tm, 1), 0)
    g1 = jnp.where(row == 0, halo[7:8, :], pltpu.roll(gate, 1, 0))
    g2 = jnp.where(row == 0, halo[6:7, :], jnp.where(row == 1, halo[7:8, :], pltpu.roll(gate, 2, 0)))
    cw = cw_ref[...]
    conv = cw[0:1, :] * g2 + cw[1:2, :] * g1 + cw[2:3, :] * gate + cb_ref[...]
    act = (conv * jax.nn.sigmoid(conv) * val).astype(BF16)
    acc_ref[...] += _dot(act, wd_ref[...])

    @pl.when(f == FF_NF - 1)
    def _():
        o_ref[...] = _layer_norm(DEEPNORM_ALPHA * h_ref[...] + acc_ref[...], g_ref[...], b_ref[...])


def _ffn(h, halo0, w_up, conv_w, conv_b, w_down, g, b, tm, seq):
    n = h.shape[0]
    kern = functools.partial(_ffn_kernel, tiles_per_seq=seq // tm)
    return pl.pallas_call(
        kern,
        grid=(n // tm, FF_NF),
        in_specs=[
            pl.BlockSpec((tm, D_MODEL), lambda i, f: (i, 0)),
            pl.BlockSpec((1, 8, FF_TF), lambda i, f: (f, 0, 0)),
            pl.BlockSpec((D_MODEL, FF_TF), lambda i, f: (0, f)),
            pl.BlockSpec((D_MODEL, FF_TF), lambda i, f: (0, FF_NF + f)),
            pl.BlockSpec((3, FF_TF), lambda i, f: (0, f)),
            pl.BlockSpec((1, FF_TF), lambda i, f: (0, f)),
            pl.BlockSpec((FF_TF, D_MODEL), lambda i, f: (f, 0)),
            pl.BlockSpec((1, D_MODEL), lambda i, f: (0, 0)),
            pl.BlockSpec((1, D_MODEL), lambda i, f: (0, 0)),
        ],
        out_specs=pl.BlockSpec((tm, D_MODEL), lambda i, f: (i, 0)),
        out_shape=jax.ShapeDtypeStruct((n, D_MODEL), F32),
        scratch_shapes=[
            pltpu.VMEM((tm, D_MODEL), BF16),
            pltpu.VMEM((tm, D_MODEL), F32),
            pltpu.VMEM((FF_NF, 8, FF_TF), F32),
        ],
        compiler_params=_cparams(("arbitrary", "arbitrary")),
        name="conv_ffn",
    )(h, halo0, w_up, w_up, conv_w, conv_b, w_down, g, b)


def _group_major(a):
    parts = [a[..., s * RWKV_WIDTH:(s + 1) * RWKV_WIDTH] for s in range(3)]
    out = []
    for gi in range(N_GROUPS):
        out += [p[..., gi * GROUP:(gi + 1) * GROUP] for p in parts]
    return jnp.concatenate(out, axis=-1)


def _pad_cols(a, width):
    return jnp.pad(a, [(0, 0)] * (a.ndim - 1) + [(0, width - a.shape[-1])])


def _pad_rows(a, rows):
    return jnp.pad(a, [(0, rows - a.shape[0])] + [(0, 0)] * (a.ndim - 1))


def _lora_slots(a):
    c = 3 * RWKV_WIDTH
    wl = a[..., c:c + DECAY_LORA]
    al = a[..., c + DECAY_LORA:c + DECAY_LORA + ICLR_LORA]
    gl = a[..., c + DECAY_LORA + ICLR_LORA:c + DECAY_LORA + ICLR_LORA + GATE_LORA]
    return jnp.concatenate([_pad_cols(wl, LORA_W), _pad_cols(al, LORA_W), _pad_cols(gl, LORA_G)], axis=-1)


def kernel(x, meta_tokens, emb_ln_g, emb_ln_b, w_in, rwkv_mu, rwkv_w0, rwkv_w2, rwkv_a0, rwkv_a2, rwkv_g2, rwkv_k_k, rwkv_k_a, rwkv_r_k, rwkv_gn_g, rwkv_gn_b, sb_norm_g, w_out, ln1_g, ln1_b, ffn_w_up, ffn_conv_w, ffn_conv_b, ffn_w_down, ln2_g, ln2_b):
    batch, seq, d = x.shape
    assert d == D_MODEL and seq % 512 == 0 and w_in.shape[0] == 1
    n_rw = 3 * RWKV_WIDTH + DECAY_LORA + ICLR_LORA + GATE_LORA
    row2 = lambda a: a.reshape(1, -1)
    grp3 = lambda a: a.reshape(N_GROUPS, 1, GROUP)

    wi = w_in[0]
    sb_cols = wi[:, n_rw:]
    sb_cols = jnp.concatenate([sb_cols[:, :SB_WIDTH] * (HEAD_DIM ** -0.5), sb_cols[:, SB_WIDTH:]], axis=1)
    w1 = jnp.concatenate([_group_major(wi[:, :3 * RWKV_WIDTH]), _lora_slots(wi), sb_cols], axis=1).astype(BF16)
    mu = rwkv_mu[0]
    prm = {
        "mu_m": row2(_group_major(mu[:3 * RWKV_WIDTH])),
        "mu_l": row2(_lora_slots(mu)),
        "w0": row2(rwkv_w0[0]),
        "w2": _pad_rows(rwkv_w2[0], LORA_W).astype(BF16),
        "a0": row2(rwkv_a0[0]),
        "a2": _pad_rows(rwkv_a2[0], LORA_W).astype(BF16),
        "g2": _pad_rows(rwkv_g2[0], LORA_G).astype(BF16),
        "k_k": row2(rwkv_k_k[0]),
        "k_a": row2(rwkv_k_a[0]),
        "r_k": row2(rwkv_r_k[0]),
        "gn_g": row2(rwkv_gn_g[0]),
        "gn_b": row2(rwkv_gn_b[0]),
    }
    w_o = w_out[0].astype(BF16)
    w_up = ffn_w_up[0].astype(BF16)
    w_dn = ffn_w_down[0].astype(BF16)
    g0, b0 = row2(emb_ln_g), row2(emb_ln_b)
    g1, b1 = row2(ln1_g[0]), row2(ln1_b[0])
    g2, b2 = row2(ln2_g[0]), row2(ln2_b[0])
    sbg = row2(sb_norm_g[0])

    xm = _pad_rows(meta_tokens.astype(x.dtype), META_PAD)
    h_m, pr_m, ps_m = _ln_inproj(xm, g0, b0, w1, META_PAD)
    ysb_m = _sb_attention_meta(ps_m, sbg)
    zeros_state = jnp.zeros((N_GROUPS, GROUP, GROUP), F32)
    yr_m, s_meta = _rwkv(pr_m, jnp.zeros((8, RWKV_PCOLS), F32), 0, zeros_state, prm, 1, 1, N_META)
    h1_m = _outproj_ln(_pad_rows(yr_m, META_PAD), ysb_m, h_m, w_o, g1, b1, META_PAD)
    halo0 = _gate_halo(h1_m, w_up)

    xr = x.reshape(batch * seq, d)
    h0, pr, ps = _ln_inproj(xr, g0, b0, w1, 512)
    y_sb = _sb_attention(ps, ps_m, sbg, batch, seq)
    y_rw, _ = _rwkv(pr, pr_m, N_META // 8 - 1, s_meta, prm, batch, seq // CHUNK, CHUNK)
    h1 = _outproj_ln(y_rw, y_sb, h0, w_o, g1, b1, 512)
    out = _ffn(h1, halo0, w_up, ffn_conv_w[0], row2(ffn_conv_b[0]), w_dn, g2, b2, 512, seq)
    return out.reshape(batch, seq, d)
```

```python
import functools

import jax
import jax.numpy as jnp
from jax import lax
from jax.experimental import pallas as pl
from jax.experimental.pallas import tpu as pltpu

F32 = jnp.float32
BF16 = jnp.bfloat16

D_MODEL = 2048
N_META = 16
HEAD_DIM = 64
RWKV_WIDTH = 1024
SB_WIDTH = 1024
DECAY_LORA = 64
ICLR_LORA = 64
GATE_LORA = 160
D_FF = 5632
DEPTH = 1
DEEPNORM_ALPHA = (2.0 * DEPTH) ** 0.25
LN_EPS = 1e-5
GN_EPS = 64e-5
RMS_EPS = 1e-6

LANES = 128
MXU_DIM = 256

GROUP = MXU_DIM
N_GROUPS = RWKV_WIDTH // GROUP
HEADS_PER_GROUP = GROUP // HEAD_DIM
CHUNK = 64
LOG2_CHUNK = 6
LOG2_HEAD = 6
RWKV_BATCH_PER_STEP = 4
RWKV_ROWS_PER_STREAM = 2
RWKV_STAGGER = 0
LORA_W = 128
LORA_G = 256
LORA_COLS = 2 * LORA_W + LORA_G
RWKV_PCOLS = 3 * RWKV_WIDTH + LORA_COLS
SB_PCOLS = 3 * SB_WIDTH
SB_BLOCK = 128
META_PAD = 128

VMEM_LIMIT = 56 * 1024 * 1024


def _cparams(sem, flags=None):
    return pltpu.CompilerParams(dimension_semantics=sem, vmem_limit_bytes=VMEM_LIMIT, flags=flags)


def _layer_norm(x, g, b):
    mu = jnp.mean(x, axis=-1, keepdims=True)
    xc = x - mu
    var = jnp.mean(xc * xc, axis=-1, keepdims=True)
    return xc * lax.rsqrt(var + LN_EPS) * g + b


def _dot(a, b):
    return jnp.dot(a, b, preferred_element_type=F32)


def _dot_nt(a, b):
    return lax.dot_general(a, b, (((1,), (1,)), ((), ())), preferred_element_type=F32)


def _split(x):
    hi = x.astype(BF16)
    lo = (x - hi.astype(F32)).astype(BF16)
    return hi, lo


def _neg_abs(x):
    bits = lax.bitcast_convert_type(x, jnp.uint32) | jnp.uint32(0x80000000)
    return lax.bitcast_convert_type(bits, F32)


def _sigmoid(x):
    return 1.0 / (1.0 + jnp.exp(-x))


def _each(f, *lists):
    return [f(*args) for args in zip(*lists)]


IN_TN = 512
IN_NR = RWKV_PCOLS // IN_TN
IN_NS = SB_PCOLS // IN_TN


def _ln_inproj_kernel(x_ref, g_ref, b_ref, w_ref, pr_ref, ps_ref, hb_ref):
    j = pl.program_id(1)

    @pl.when(j == 0)
    def _():
        hb_ref[...] = _layer_norm(x_ref[...], g_ref[...], b_ref[...]).astype(BF16)

    acc = _dot(hb_ref[...], w_ref[...])

    @pl.when(j < IN_NR)
    def _():
        pr_ref[...] = acc

    @pl.when(j >= IN_NR)
    def _():
        ps_ref[...] = acc.astype(BF16)


def _ln_inproj(x, g, b, w, tm):
    n = x.shape[0]
    return pl.pallas_call(
        _ln_inproj_kernel,
        grid=(n // tm, IN_NR + IN_NS),
        in_specs=[
            pl.BlockSpec((tm, D_MODEL), lambda i, j: (i, 0)),
            pl.BlockSpec((1, D_MODEL), lambda i, j: (0, 0)),
            pl.BlockSpec((1, D_MODEL), lambda i, j: (0, 0)),
            pl.BlockSpec((D_MODEL, IN_TN), lambda i, j: (0, j)),
        ],
        out_specs=[
            pl.BlockSpec((tm, IN_TN), lambda i, j: (i, jnp.minimum(j, IN_NR - 1))),
            pl.BlockSpec((tm, IN_TN), lambda i, j: (i, jnp.maximum(j - IN_NR, 0))),
        ],
        out_shape=[
            jax.ShapeDtypeStruct((n, RWKV_PCOLS), F32),
            jax.ShapeDtypeStruct((n, SB_PCOLS), BF16),
        ],
        scratch_shapes=[pltpu.VMEM((tm, D_MODEL), BF16)],
        compiler_params=_cparams(("arbitrary", "arbitrary")),
        name="ln_inproj",
    )(x, g, b, w)


SB_TQ = 256
SB_W = MXU_DIM
SB_HEADS = SB_W // HEAD_DIM


def _sb_setup(carry_ref, acc_ref, tq):
    r = lax.broadcasted_iota(jnp.int32, (2 * SB_BLOCK, 2 * SB_BLOCK), 0) & (SB_BLOCK - 1)
    c = lax.broadcasted_iota(jnp.int32, (2 * SB_BLOCK, 2 * SB_BLOCK), 1)
    u = ((c >= SB_BLOCK) | (r > c)).astype(BF16)
    rb = lax.broadcasted_iota(jnp.int32, (SB_HEADS * SB_BLOCK, SB_W), 0)
    cb = lax.broadcasted_iota(jnp.int32, (SB_HEADS * SB_BLOCK, SB_W), 1)
    bd_mask = (rb >> 7) == (cb >> LOG2_HEAD)
    carry_ref[...] = jnp.zeros_like(carry_ref)
    acc_ref[...] = jnp.zeros_like(acc_ref)
    row = lax.broadcasted_iota(jnp.int32, (tq, SB_BLOCK), 0)
    col = lax.broadcasted_iota(jnp.int32, (tq, SB_BLOCK), 1)
    return u, bd_mask, row, col


def _sb_block_diag(x, bd_mask):
    return jnp.where(bd_mask, jnp.concatenate([x] * SB_HEADS, axis=0), jnp.zeros((), BF16))


def _sb_weights(z_all, mask, u, carry_ref):
    z = [z_all[:, h * SB_BLOCK:(h + 1) * SB_BLOCK] for h in range(SB_HEADS)]
    t = _each(lambda a: jnp.log(1.0 + jnp.exp(_neg_abs(a))), z)
    log_beta = _each(lambda a, b: jnp.minimum(a, 0.0) - b, z, t)
    log_keep = _each(lambda a, b: a - b, log_beta, z)
    if mask is not None:
        log_keep = _each(lambda a: jnp.where(mask, a, 0.0), log_keep)
    hl = _each(lambda a: jnp.concatenate(_split(a), axis=1), log_keep)
    at = _each(lambda a: _dot(a, u), hl)
    carry = [carry_ref[h] for h in range(SB_HEADS)]
    w = _each(lambda a, b, c: jnp.exp(a + b[:, :SB_BLOCK] + c), log_beta, at, carry)
    if mask is not None:
        w = _each(lambda a: jnp.where(mask, a, 0.0), w)
    for h in range(SB_HEADS):
        carry_ref[h] = carry[h] + at[h][:, SB_BLOCK:]
    return jnp.concatenate([a.astype(BF16) for a in w], axis=1)


def _sb_step(q, kb, vb, mask, u, bd_mask, carry_ref, acc_ref):
    w = _sb_weights(_dot_nt(q, _sb_block_diag(kb, bd_mask)), mask, u, carry_ref)
    acc_ref[...] += _dot(w, _sb_block_diag(vb, bd_mask))


def _sb_finish(acc_ref, g_ref, o_ref):
    tq = acc_ref.shape[0]
    lane = lax.broadcasted_iota(jnp.int32, (tq, LANES), 1)
    hm = (lane < HEAD_DIM, lane >= HEAD_DIM)
    for hp in range(SB_W // LANES):
        sl = slice(hp * LANES, (hp + 1) * LANES)
        acc = acc_ref[:, sl]
        sq = acc * acc
        ms = [jnp.sum(jnp.where(m, sq, 0.0), axis=-1, keepdims=True) * (1.0 / HEAD_DIM) for m in hm]
        ms = jnp.where(hm[0], ms[0], ms[1])
        o_ref[:, sl] = (acc * lax.rsqrt(ms + RMS_EPS) * g_ref[:, sl]).astype(o_ref.dtype)


def _sb_kernel(q_ref, k_ref, v_ref, km_ref, vm_ref, g_ref, o_ref, carry_ref, acc_ref,
               kbd_ref, vbd_ref, z0_ref, z1_ref, w0_ref, w1_ref, *, tq, n_blocks):
    qi = pl.program_id(2)
    assert tq == 2 * SB_BLOCK
    u, bd_mask, row, col = _sb_setup(carry_ref, acc_ref, tq)
    q = q_ref[...]

    @pl.when(qi == 0)
    def _():
        def fill(j, carry):
            s = pl.multiple_of(j * SB_BLOCK, SB_BLOCK)
            kbd_ref[j] = _sb_block_diag(k_ref[pl.ds(s, SB_BLOCK), :], bd_mask)
            vbd_ref[j] = _sb_block_diag(v_ref[pl.ds(s, SB_BLOCK), :], bd_mask)
            return carry
        lax.fori_loop(0, n_blocks, fill, 0)
        kbd_ref[n_blocks] = _sb_block_diag(km_ref[...], bd_mask)
        vbd_ref[n_blocks] = _sb_block_diag(vm_ref[...], bd_mask)

    def logits(blk, z_ref):
        z_ref[...] = _dot_nt(q, kbd_ref[blk])

    def weights(z_ref, w_ref, mask):
        w_ref[...] = _sb_weights(z_ref[...], mask, u, carry_ref)

    def pv(blk, w_ref):
        acc_ref[...] += _dot(w_ref[...], vbd_ref[blk])

    top = 2 * qi + 1
    logits(top, z0_ref)
    logits(top - 1, z1_ref)
    weights(z0_ref, w0_ref, col + SB_BLOCK < row)
    logits(jnp.where(qi == 0, n_blocks, top - 2), z0_ref)
    pv(top, w0_ref)
    weights(z1_ref, w1_ref, col < row)

    def body(m, carry):
        b0 = top - 2 - 2 * m
        logits(b0 - 1, z1_ref)
        pv(b0 + 1, w1_ref)
        weights(z0_ref, w0_ref, None)
        logits(jnp.where(m == qi - 1, n_blocks, b0 - 2), z0_ref)
        pv(b0, w0_ref)
        weights(z1_ref, w1_ref, None)
        return carry

    lax.fori_loop(0, qi, body, 0)
    pv(0, w1_ref)
    weights(z0_ref, w0_ref, col < N_META)
    pv(n_blocks, w0_ref)
    _sb_finish(acc_ref, g_ref, o_ref)


def _sb_meta_kernel(q_ref, k_ref, v_ref, g_ref, o_ref, carry_ref, acc_ref):
    u, bd_mask, row, col = _sb_setup(carry_ref, acc_ref, META_PAD)
    _sb_step(q_ref[...], k_ref[...], v_ref[...], col < row, u, bd_mask, carry_ref, acc_ref)
    _sb_finish(acc_ref, g_ref, o_ref)


def _sb_scratch(tq):
    return [pltpu.VMEM((SB_HEADS, tq, SB_BLOCK), F32), pltpu.VMEM((tq, SB_W), F32)]


def _sb_attention(ps, ps_meta, norm_g, batch, seq):
    w = SB_W
    n_g = SB_WIDTH // w
    nq = seq // SB_TQ
    n_blocks = seq // SB_BLOCK
    kern = functools.partial(_sb_kernel, tq=SB_TQ, n_blocks=n_blocks)
    bd_rows = SB_HEADS * SB_BLOCK
    pipeline_scratch = [
        pltpu.VMEM((n_blocks + 1, bd_rows, w), BF16),
        pltpu.VMEM((n_blocks + 1, bd_rows, w), BF16),
        pltpu.VMEM((SB_TQ, bd_rows), F32), pltpu.VMEM((SB_TQ, bd_rows), F32),
        pltpu.VMEM((SB_TQ, bd_rows), BF16), pltpu.VMEM((SB_TQ, bd_rows), BF16),
    ]
    return pl.pallas_call(
        kern,
        grid=(batch, n_g, nq),
        in_specs=[
            pl.BlockSpec((SB_TQ, w), lambda b, h, q: (b * nq + q, h)),
            pl.BlockSpec((seq, w), lambda b, h, q: (b, n_g + h)),
            pl.BlockSpec((seq, w), lambda b, h, q: (b, 2 * n_g + h)),
            pl.BlockSpec((META_PAD, w), lambda b, h, q: (0, n_g + h)),
            pl.BlockSpec((META_PAD, w), lambda b, h, q: (0, 2 * n_g + h)),
            pl.BlockSpec((1, w), lambda b, h, q: (0, h)),
        ],
        out_specs=pl.BlockSpec((SB_TQ, w), lambda b, h, q: (b * nq + q, h)),
        out_shape=jax.ShapeDtypeStruct((batch * seq, SB_WIDTH), BF16),
        scratch_shapes=_sb_scratch(SB_TQ) + pipeline_scratch,
        compiler_params=_cparams(("arbitrary", "arbitrary", "arbitrary")),
        name="sb_attention",
    )(ps, ps, ps, ps_meta, ps_meta, norm_g)


def _sb_attention_meta(ps_meta, norm_g):
    w = SB_W
    n_g = SB_WIDTH // w
    return pl.pallas_call(
        _sb_meta_kernel,
        grid=(n_g,),
        in_specs=[
            pl.BlockSpec((META_PAD, w), lambda h: (0, h)),
            pl.BlockSpec((META_PAD, w), lambda h: (0, n_g + h)),
            pl.BlockSpec((META_PAD, w), lambda h: (0, 2 * n_g + h)),
            pl.BlockSpec((1, w), lambda h: (0, h)),
        ],
        out_specs=pl.BlockSpec((META_PAD, w), lambda h: (0, h)),
        out_shape=jax.ShapeDtypeStruct((META_PAD, SB_WIDTH), BF16),
        scratch_shapes=_sb_scratch(META_PAD),
        compiler_params=_cparams(("arbitrary",)),
        name="sb_attention_meta",
    )(ps_meta, ps_meta, ps_meta, norm_g)


def _bd(x, seg_ones):
    return jnp.concatenate([x.astype(BF16)] * HEADS_PER_GROUP, axis=0) * seg_ones


def _seg_sum(a, seg_ones):
    return _dot(a.astype(BF16), seg_ones)


def _mm_bd(a, b, seg_ones):
    return _dot(a.astype(BF16), _bd(b, seg_ones))


def _rwkv_kernel(pm_ref, pl_ref, prev_m_ref, prev_l_ref, s0_ref, mu_m_ref, mu_l_ref,
                 w0_ref, w2_ref, a0_ref, a2_ref, g2_ref, kk_ref, ka_ref, rk_ref, gng_ref, gnb_ref,
                 y_ref, s_out_ref, s_ref, carry_m_ref, carry_l_ref, *, n_valid):
    c = pl.program_id(1)
    L = CHUNK
    nb = pm_ref.shape[0]

    @pl.when(c == 0)
    def _():
        for bi in range(nb):
            s_ref[bi * N_GROUPS:(bi + 1) * N_GROUPS] = s0_ref[...]
            carry_m_ref[bi] = prev_m_ref[...]
            carry_l_ref[bi] = prev_l_ref[...]

    row = lax.broadcasted_iota(jnp.int32, (L, 1), 0)
    first = row == 0

    def token_shift(p, prev_last, mu):
        shifted = jnp.where(first, prev_last, pltpu.roll(p, 1, 0))
        return p + (shifted - p) * mu

    r4 = lax.broadcasted_iota(jnp.int32, (4 * L, GROUP), 0)
    c4 = lax.broadcasted_iota(jnp.int32, (4 * L, GROUP), 1)
    bd_mask = (r4 >> LOG2_CHUNK) == (c4 >> LOG2_HEAD)
    seg_ones = bd_mask.astype(BF16)
    ri = lax.broadcasted_iota(jnp.int32, (L, L), 0)
    ci = lax.broadcasted_iota(jnp.int32, (L, L), 1)
    tri_incl = (ci <= ri).astype(BF16)
    r2 = lax.broadcasted_iota(jnp.int32, (2 * L, 4 * L), 0)
    c2 = lax.broadcasted_iota(jnp.int32, (2 * L, 4 * L), 1) & (L - 1)
    low_mask = c2 < (r2 & (L - 1)) + (r2 >> LOG2_CHUNK)
    rl = lax.broadcasted_iota(jnp.int32, (L, 4 * L), 0)
    cl = lax.broadcasted_iota(jnp.int32, (L, 4 * L), 1) & (L - 1)
    eye_t = (rl == cl).astype(F32)
    consts = (row, first, bd_mask, seg_ones, tri_incl, low_mask, eye_t)

    gsl = [slice(g * GROUP, (g + 1) * GROUP) for g in range(N_GROUPS)]
    xm, lora = [], ([], [], [])
    for bi in range(nb):
        pl_blk = pl_ref[bi]
        xl = token_shift(pl_blk, carry_l_ref[bi, 7:8, :], mu_l_ref[...])
        carry_l_ref[bi] = pl_blk[L - 8:, :]
        parts = (jnp.tanh(xl[:, 0:LORA_W]).astype(BF16), xl[:, LORA_W:2 * LORA_W].astype(BF16),
                 _sigmoid(xl[:, 2 * LORA_W:]).astype(BF16))
        pm_blk = pm_ref[bi]
        xm_all = token_shift(pm_blk, carry_m_ref[bi, 7:8, :], mu_m_ref[...])
        carry_m_ref[bi] = pm_blk[L - 8:, :]
        xm += [xm_all[:, g * 3 * GROUP:(g + 1) * 3 * GROUP] for g in range(N_GROUPS)]
        for dst, part in zip(lora, parts):
            dst += [part] * N_GROUPS
    prm = [[ref[:, gs] for gs in gsl] for ref in
           (w0_ref, w2_ref, a0_ref, a2_ref, g2_ref, kk_ref, ka_ref, rk_ref, gng_ref, gnb_ref)]
    rows = min(nb, RWKV_ROWS_PER_STREAM)
    per = rows * N_GROUPS
    streams = []
    for si in range(nb // rows):
        sl = slice(si * per, (si + 1) * per)
        streams.append(_rwkv_groups(xm[sl], [part[sl] for part in lora], [p * rows for p in prm],
                                    [s_ref[i] for i in range(sl.start, sl.stop)], consts, n_valid))
    for si, (ys, s_new) in enumerate(_run_staggered(streams, RWKV_STAGGER)):
        for j in range(per):
            i = si * per + j
            s_ref[i] = s_new[j]
            s_out_ref[i] = s_new[j]
            y_ref[i // N_GROUPS, :, gsl[i % N_GROUPS]] = ys[j].astype(y_ref.dtype)


def _run_staggered(streams, lag):
    results = [None] * len(streams)
    tick = 0
    while any(r is None for r in results):
        for i, stream in enumerate(streams):
            if results[i] is None and tick >= i * lag:
                try:
                    next(stream)
                except StopIteration as stop:
                    results[i] = stop.value
        tick += 1
    return results


def _rwkv_groups(xm, lora, prm, s_t, consts, n_valid):
    L = CHUNK
    row, first, bd_mask, seg_ones, tri_incl, low_mask, eye_t = consts
    w0, w2, a0, a2, g2, k_k, k_a, r_k, gn_g, gn_b = prm
    tanh_wl, al, sig_gl = lora
    r = [x[:, 0:GROUP] for x in xm]
    k = [x[:, GROUP:2 * GROUP] for x in xm]
    v = [x[:, 2 * GROUP:3 * GROUP] for x in xm]

    dw = _each(lambda b, x, w: b + _dot(x, w), w0, tanh_wl, w2)
    w_log = _each(lambda d: jnp.minimum(d, 0.0) - jnp.log(1.0 + jnp.exp(_neg_abs(d))) - 0.5, dw)
    logw = _each(lambda x: -jnp.exp(x), w_log)
    iclr = _each(lambda b, x, w: _sigmoid(b + _dot(x, w)), a0, al, a2)
    gate = _each(_dot, sig_gl, g2)
    yield

    kk = _each(lambda a, b: a * b, k, k_k)
    kk = _each(lambda x: x * lax.rsqrt(jnp.maximum(_seg_sum(x * x, seg_ones), 1e-24)), kk)
    kmod = _each(lambda x, i, a: x * (1.0 + (i - 1.0) * a), k, iclr, k_a)

    if n_valid < L:
        valid = row < n_valid
        zero_pad = lambda x: jnp.where(valid, x, 0.0)
        kk, kmod, v, logw = _each(zero_pad, kk), _each(zero_pad, kmod), _each(zero_pad, v), _each(zero_pad, logw)

    yield
    cum = _each(lambda x: _mm_hi_lhs_const(tri_incl, x), logw)
    yield
    p_in = _each(jnp.exp, cum)
    p_ex = _each(lambda c, w: jnp.exp(c - w), cum, logw)
    p_inv = _each(lambda c: jnp.exp(-c), cum)
    p_tail = _each(lambda c: jnp.exp(c[L - 1:L, :] - c), cum)
    p_last = _each(lambda c: jnp.exp(c[L - 1:L, :]), cum)
    yield
    at = _each(lambda a, p: -a * p, kk, p_ex)
    kb = _each(lambda a, i: a * i, kk, iclr)
    bt = _each(lambda a, p: (a * p).astype(BF16), kb, p_inv)
    kt = _each(lambda a, p: (a * p).astype(BF16), kmod, p_inv)
    rt = _each(lambda a, p: a * p, r, p_in)
    yield

    ar = _each(lambda a, b: jnp.concatenate([a, b], axis=0).astype(BF16), at, rt)
    abr = _each(lambda a, b: jnp.where(low_mask, _dot_nt(a, _bd(b, seg_ones)), 0.0), ar, bt)
    akr = _each(lambda a, b: jnp.where(low_mask, _dot_nt(a, _bd(b, seg_ones)), 0.0), ar, kt)
    yield
    sa = _each(lambda a, s: _dot_nt(a, s.astype(BF16)), ar, s_t)
    bd_v = _each(lambda x: _bd(x, seg_ones), v)
    x = _each(lambda s, a, b: s[:L] + _dot(a[:L].astype(BF16), b), sa, akr, bd_v)
    yield

    acc = [eye_t] * len(xm)
    pw = [a[:L] for a in abr]
    for lvl in range(LOG2_CHUNK):
        if lvl < LOG2_CHUNK - 1:
            res = _each(lambda a, p: _mm_bd(jnp.concatenate([a, p], axis=0), p, seg_ones), acc, pw)
            acc = _each(lambda a, rr: a + rr[:L], acc, res)
            pw = [rr[L:] for rr in res]
        else:
            acc = _each(lambda a, p: a + _mm_bd(a, p, seg_ones), acc, pw)
        yield
    u = _each(lambda a, b: _mm_bd(a, b, seg_ones), acc, x)
    yield

    rbk = _each(lambda a, b: jnp.concatenate([a[L:], b[L:]], axis=1).astype(BF16), abr, akr)
    uv_bd = _each(lambda a, b: jnp.concatenate([_bd(a, seg_ones), b], axis=0), u, bd_v)
    y = _each(lambda s, a, b: s[L:] + _dot(a, b), sa, rbk, uv_bd)
    yield

    uv_t = _each(lambda a, b: jnp.transpose(jnp.concatenate([a, b], axis=0)).astype(BF16), u, v)
    bk_tail = _each(lambda a, b, p: jnp.concatenate([a * p, b * p], axis=0).astype(BF16), kb, kmod, p_tail)
    s_new = _each(lambda s, p, a, b: s * p + jnp.where(bd_mask, _dot(a, b), 0.0), s_t, p_last, uv_t, bk_tail)
    yield

    mean = _each(lambda a: _seg_sum(a, seg_ones) * (1.0 / HEAD_DIM), y)
    yc = _each(lambda a, m: a - m, y, mean)
    var = _each(lambda a: _seg_sum(a * a, seg_ones) * (1.0 / HEAD_DIM), yc)
    yn = _each(lambda a, s, g_, b_: a * lax.rsqrt(s + GN_EPS) * g_ + b_, yc, var, gn_g, gn_b)
    bonus = _each(lambda a, b, c, d: _seg_sum(a * b * c, seg_ones) * d, r, kmod, r_k, v)
    out = _each(lambda a, b, c: (a + b) * c, yn, bonus, gate)
    return out, s_new


def _mm_hi_lhs_const(a_bf16, b):
    hi, lo = _split(b)
    return _dot(a_bf16, hi) + _dot(a_bf16, lo)


def _rwkv(p, prev, prev_blk, s0, prm, nb, n_valid):
    batch, rows, _ = p.shape
    n_chunks = rows // CHUNK
    n_main = 3 * RWKV_WIDTH
    lora_blk = n_main // LORA_COLS
    row_map = lambda b, c: (b, c, 0)
    whole = lambda b, c: (0, 0)
    vec = pl.BlockSpec((1, RWKV_WIDTH), whole)
    kern = functools.partial(_rwkv_kernel, n_valid=n_valid)
    return pl.pallas_call(
        kern,
        grid=(batch // nb, n_chunks),
        in_specs=[
            pl.BlockSpec((nb, CHUNK, n_main), row_map),
            pl.BlockSpec((nb, CHUNK, LORA_COLS), lambda b, c: (b, c, lora_blk)),
            pl.BlockSpec((8, n_main), lambda b, c: (prev_blk, 0)),
            pl.BlockSpec((8, LORA_COLS), lambda b, c: (prev_blk, lora_blk)),
            pl.BlockSpec((N_GROUPS, GROUP, GROUP), lambda b, c: (0, 0, 0)),
            pl.BlockSpec((1, n_main), whole),
            pl.BlockSpec((1, LORA_COLS), whole),
            vec,
            pl.BlockSpec((LORA_W, RWKV_WIDTH), whole),
            vec,
            pl.BlockSpec((LORA_W, RWKV_WIDTH), whole),
            pl.BlockSpec((LORA_G, RWKV_WIDTH), whole),
            vec, vec, vec, vec, vec,
        ],
        out_specs=[
            pl.BlockSpec((nb, CHUNK, RWKV_WIDTH), row_map),
            pl.BlockSpec((nb * N_GROUPS, GROUP, GROUP), lambda b, c: (b, 0, 0)),
        ],
        out_shape=[
            jax.ShapeDtypeStruct((batch, rows, RWKV_WIDTH), BF16),
            jax.ShapeDtypeStruct((batch * N_GROUPS, GROUP, GROUP), F32),
        ],
        scratch_shapes=[
            pltpu.VMEM((nb * N_GROUPS, GROUP, GROUP), F32),
            pltpu.VMEM((nb, 8, n_main), F32),
            pltpu.VMEM((nb, 8, LORA_COLS), F32),
        ],
        compiler_params=_cparams(("arbitrary", "arbitrary")),
        name="rwkv7",
    )(p, p, prev, prev, s0, prm["mu_m"], prm["mu_l"], prm["w0"], prm["w2"], prm["a0"], prm["a2"],
      prm["g2"], prm["k_k"], prm["k_a"], prm["r_k"], prm["gn_g"], prm["gn_b"])


def _outproj_ln_kernel(yr_ref, ys_ref, x_ref, w_ref, g0_ref, b0_ref, g_ref, b_ref, o_ref):
    h = _layer_norm(x_ref[...], g0_ref[...], b0_ref[...])
    mix = _dot(yr_ref[...], w_ref[0:RWKV_WIDTH, :]) + _dot(ys_ref[...], w_ref[RWKV_WIDTH:, :])
    o_ref[...] = _layer_norm(DEEPNORM_ALPHA * h + mix, g_ref[...], b_ref[...])


def _outproj_ln(yr, ys, x, w, g0, b0, g, b, tm):
    n = x.shape[0]
    return pl.pallas_call(
        _outproj_ln_kernel,
        grid=(n // tm,),
        in_specs=[
            pl.BlockSpec((tm, RWKV_WIDTH), lambda i: (i, 0)),
            pl.BlockSpec((tm, SB_WIDTH), lambda i: (i, 0)),
            pl.BlockSpec((tm, D_MODEL), lambda i: (i, 0)),
            pl.BlockSpec((D_MODEL, D_MODEL), lambda i: (0, 0)),
            pl.BlockSpec((1, D_MODEL), lambda i: (0, 0)),
            pl.BlockSpec((1, D_MODEL), lambda i: (0, 0)),
            pl.BlockSpec((1, D_MODEL), lambda i: (0, 0)),
            pl.BlockSpec((1, D_MODEL), lambda i: (0, 0)),
        ],
        out_specs=pl.BlockSpec((tm, D_MODEL), lambda i: (i, 0)),
        out_shape=jax.ShapeDtypeStruct((n, D_MODEL), F32),
        compiler_params=_cparams(("arbitrary",)),
        name="outproj_ln",
    )(yr, ys, x, w, g0, b0, g, b)


FF_TF = 512
FF_NF = D_FF // FF_TF


def _gate_halo_kernel(h_ref, w_ref, o_ref):
    o_ref[0] = _dot(h_ref[...].astype(BF16), w_ref[...])


def _gate_halo(h_rows, w_up):
    return pl.pallas_call(
        _gate_halo_kernel,
        grid=(FF_NF,),
        in_specs=[
            pl.BlockSpec((8, D_MODEL), lambda f: (N_META // 8 - 1, 0)),
            pl.BlockSpec((D_MODEL, FF_TF), lambda f: (0, f)),
        ],
        out_specs=pl.BlockSpec((1, 8, FF_TF), lambda f: (f, 0, 0)),
        out_shape=jax.ShapeDtypeStruct((FF_NF, 8, FF_TF), F32),
        compiler_params=_cparams(("arbitrary",)),
        name="ffn_gate_halo",
    )(h_rows, w_up)


def _ffn_kernel(h_ref, halo0_ref, wg_ref, wv_ref, cw_ref, cb_ref, wd_ref, g_ref, b_ref, o_ref,
                hb_ref, acc_ref, halo_ref, *, tiles_per_seq):
    i = pl.program_id(0)
    f = pl.program_id(1)
    tm = h_ref.shape[0]

    @pl.when(f == 0)
    def _():
        hb_ref[...] = h_ref[...].astype(BF16)
        acc_ref[...] = jnp.zeros_like(acc_ref)

    @pl.when((i % tiles_per_seq) == 0)
    def _():
        halo_ref[f] = halo0_ref[0]

    hb = hb_ref[...]
    row = lax.broadcasted_iota(jnp.int32, (tm, 1), 0)
    halves = [slice(s * MXU_DIM, (s + 1) * MXU_DIM) for s in range(FF_TF // MXU_DIM)]
    gate = [_dot(hb, wg_ref[:, cs]) for cs in halves]
    val = [_dot(hb, wv_ref[:, cs]) for cs in halves]
    down = []
    for cs, g, v in zip(halves, gate, val):
        halo = halo_ref[f, :, cs]
        halo_ref[f, :, cs] = g[tm - 8:, :]
        g1 = jnp.where(row == 0, halo[7:8, :], pltpu.roll(g, 1, 0))
        g2 = jnp.where(row == 0, halo[6:7, :], jnp.where(row == 1, halo[7:8, :], pltpu.roll(g, 2, 0)))
        conv = cw_ref[0:1, cs] * g2 + cw_ref[1:2, cs] * g1 + cw_ref[2:3, cs] * g + cb_ref[:, cs]
        act = (conv * _sigmoid(conv) * v).astype(BF16)
        down.append(_dot(act, wd_ref[cs, :]))
    acc_ref[...] += functools.reduce(lambda a, b: a + b, down)

    @pl.when(f == FF_NF - 1)
    def _():
        o_ref[...] = _layer_norm(DEEPNORM_ALPHA * h_ref[...] + acc_ref[...], g_ref[...], b_ref[...])


def _ffn(h, halo0, w_up, conv_w, conv_b, w_down, g, b, tm, seq):
    n = h.shape[0]
    kern = functools.partial(_ffn_kernel, tiles_per_seq=seq // tm)
    return pl.pallas_call(
        kern,
        grid=(n // tm, FF_NF),
        in_specs=[
            pl.BlockSpec((tm, D_MODEL), lambda i, f: (i, 0)),
            pl.BlockSpec((1, 8, FF_TF), lambda i, f: (f, 0, 0)),
            pl.BlockSpec((D_MODEL, FF_TF), lambda i, f: (0, f)),
            pl.BlockSpec((D_MODEL, FF_TF), lambda i, f: (0, FF_NF + f)),
            pl.BlockSpec((3, FF_TF), lambda i, f: (0, f)),
            pl.BlockSpec((1, FF_TF), lambda i, f: (0, f)),
            pl.BlockSpec((FF_TF, D_MODEL), lambda i, f: (f, 0)),
            pl.BlockSpec((1, D_MODEL), lambda i, f: (0, 0)),
            pl.BlockSpec((1, D_MODEL), lambda i, f: (0, 0)),
        ],
        out_specs=pl.BlockSpec((tm, D_MODEL), lambda i, f: (i, 0)),
        out_shape=jax.ShapeDtypeStruct((n, D_MODEL), F32),
        scratch_shapes=[
            pltpu.VMEM((tm, D_MODEL), BF16),
            pltpu.VMEM((tm, D_MODEL), F32),
            pltpu.VMEM((FF_NF, 8, FF_TF), F32),
        ],
        compiler_params=_cparams(("arbitrary", "arbitrary")),
        name="conv_ffn",
    )(h, halo0, w_up, w_up, conv_w, conv_b, w_down, g, b)


def _group_major(a):
    parts = [a[..., s * RWKV_WIDTH:(s + 1) * RWKV_WIDTH] for s in range(3)]
    out = []
    for gi in range(N_GROUPS):
        out += [p[..., gi * GROUP:(gi + 1) * GROUP] for p in parts]
    return jnp.concatenate(out, axis=-1)


def _pad_cols(a, width):
    return jnp.pad(a, [(0, 0)] * (a.ndim - 1) + [(0, width - a.shape[-1])])


def _pad_rows(a, rows):
    return jnp.pad(a, [(0, rows - a.shape[0])] + [(0, 0)] * (a.ndim - 1))


def _lora_slots(a):
    c = 3 * RWKV_WIDTH
    wl = a[..., c:c + DECAY_LORA]
    al = a[..., c + DECAY_LORA:c + DECAY_LORA + ICLR_LORA]
    gl = a[..., c + DECAY_LORA + ICLR_LORA:c + DECAY_LORA + ICLR_LORA + GATE_LORA]
    return jnp.concatenate([_pad_cols(wl, LORA_W), _pad_cols(al, LORA_W), _pad_cols(gl, LORA_G)], axis=-1)


def kernel(x, meta_tokens, emb_ln_g, emb_ln_b, w_in, rwkv_mu, rwkv_w0, rwkv_w2, rwkv_a0, rwkv_a2, rwkv_g2, rwkv_k_k, rwkv_k_a, rwkv_r_k, rwkv_gn_g, rwkv_gn_b, sb_norm_g, w_out, ln1_g, ln1_b, ffn_w_up, ffn_conv_w, ffn_conv_b, ffn_w_down, ln2_g, ln2_b):
    batch, seq, d = x.shape
    assert d == D_MODEL and seq % 512 == 0 and w_in.shape[0] == 1 and batch % RWKV_BATCH_PER_STEP == 0
    n_rw = 3 * RWKV_WIDTH + DECAY_LORA + ICLR_LORA + GATE_LORA
    row2 = lambda a: a.reshape(1, -1)
    grp3 = lambda a: a.reshape(N_GROUPS, 1, GROUP)

    wi = w_in[0]
    sb_cols = wi[:, n_rw:]
    sb_cols = jnp.concatenate([sb_cols[:, :SB_WIDTH] * (HEAD_DIM ** -0.5), sb_cols[:, SB_WIDTH:]], axis=1)
    w1 = jnp.concatenate([_group_major(wi[:, :3 * RWKV_WIDTH]), _lora_slots(wi), sb_cols], axis=1).astype(BF16)
    mu = rwkv_mu[0]
    prm = {
        "mu_m": row2(_group_major(mu[:3 * RWKV_WIDTH])),
        "mu_l": row2(_lora_slots(mu)),
        "w0": row2(rwkv_w0[0]),
        "w2": _pad_rows(rwkv_w2[0], LORA_W).astype(BF16),
        "a0": row2(rwkv_a0[0]),
        "a2": _pad_rows(rwkv_a2[0], LORA_W).astype(BF16),
        "g2": _pad_rows(rwkv_g2[0], LORA_G).astype(BF16),
        "k_k": row2(rwkv_k_k[0]),
        "k_a": row2(rwkv_k_a[0]),
        "r_k": row2(rwkv_r_k[0]),
        "gn_g": row2(rwkv_gn_g[0]),
        "gn_b": row2(rwkv_gn_b[0]),
    }
    w_o = w_out[0].astype(BF16)
    w_up = ffn_w_up[0].astype(BF16)
    w_dn = ffn_w_down[0].astype(BF16)
    g0, b0 = row2(emb_ln_g), row2(emb_ln_b)
    g1, b1 = row2(ln1_g[0]), row2(ln1_b[0])
    g2, b2 = row2(ln2_g[0]), row2(ln2_b[0])
    sbg = row2(sb_norm_g[0])

    xm = _pad_rows(meta_tokens.astype(x.dtype), META_PAD)
    pr_m, ps_m = _ln_inproj(xm, g0, b0, w1, META_PAD)
    ysb_m = _sb_attention_meta(ps_m, sbg)
    zeros_state = jnp.zeros((N_GROUPS, GROUP, GROUP), F32)
    yr_m, s_meta = _rwkv(pr_m[:CHUNK].reshape(1, CHUNK, RWKV_PCOLS), jnp.zeros((8, RWKV_PCOLS), F32), 0,
                         zeros_state, prm, 1, N_META)
    h1_m = _outproj_ln(_pad_rows(yr_m[0], META_PAD), ysb_m, xm, w_o, g0, b0, g1, b1, META_PAD)
    halo0 = _gate_halo(h1_m, w_up)

    xr = x.reshape(batch * seq, d)
    pr, ps = _ln_inproj(xr, g0, b0, w1, 1024)
    y_sb = _sb_attention(ps, ps_m, sbg, batch, seq)
    y_rw, _ = _rwkv(pr.reshape(batch, seq, RWKV_PCOLS), pr_m, N_META // 8 - 1, s_meta, prm,
                    RWKV_BATCH_PER_STEP, CHUNK)
    h1 = _outproj_ln(y_rw.reshape(batch * seq, RWKV_WIDTH), y_sb, xr, w_o, g0, b0, g1, b1, 512)
    out = _ffn(h1, halo0, w_up, ffn_conv_w[0], row2(ffn_conv_b[0]), w_dn, g2, b2, 512, seq)
    return out.reshape(batch, seq, d)
```

```python
import functools

import jax
import jax.numpy as jnp
from jax import lax
from jax.experimental import pallas as pl
from jax.experimental.pallas import tpu as pltpu

F32 = jnp.float32
BF16 = jnp.bfloat16

D_MODEL = 2048
N_META = 16
HEAD_DIM = 64
RWKV_WIDTH = 1024
SB_WIDTH = 1024
DECAY_LORA = 64
ICLR_LORA = 64
GATE_LORA = 160
D_FF = 5632
DEPTH = 1
DEEPNORM_ALPHA = (2.0 * DEPTH) ** 0.25
LN_EPS = 1e-5
GN_EPS = 64e-5
RMS_EPS = 1e-6

LANES = 128
MXU_DIM = 256

GROUP = MXU_DIM
N_GROUPS = RWKV_WIDTH // GROUP
HEADS_PER_GROUP = GROUP // HEAD_DIM
CHUNK = 64
LOG2_CHUNK = 6
LOG2_HEAD = 6
RWKV_BATCH_PER_STEP = 4
LORA_W = 128
LORA_G = 256
LORA_COLS = 2 * LORA_W + LORA_G
RWKV_PCOLS = 3 * RWKV_WIDTH + LORA_COLS
SB_PCOLS = 3 * SB_WIDTH
SB_BLOCK = 128
META_PAD = 128

VMEM_LIMIT = 56 * 1024 * 1024


def _cparams(sem, flags=None):
    return pltpu.CompilerParams(dimension_semantics=sem, vmem_limit_bytes=VMEM_LIMIT, flags=flags)


def _layer_norm(x, g, b):
    mu = jnp.mean(x, axis=-1, keepdims=True)
    xc = x - mu
    var = jnp.mean(xc * xc, axis=-1, keepdims=True)
    return xc * lax.rsqrt(var + LN_EPS) * g + b


def _dot(a, b):
    return jnp.dot(a, b, preferred_element_type=F32)


def _dot_nt(a, b):
    return lax.dot_general(a, b, (((1,), (1,)), ((), ())), preferred_element_type=F32)


def _split(x):
    hi = x.astype(BF16)
    lo = (x - hi.astype(F32)).astype(BF16)
    return hi, lo


def _neg_abs(x):
    bits = lax.bitcast_convert_type(x, jnp.uint32) | jnp.uint32(0x80000000)
    return lax.bitcast_convert_type(bits, F32)


def _sigmoid(x):
    return 1.0 / (1.0 + jnp.exp(-x))


def _each(f, *lists):
    return [f(*args) for args in zip(*lists)]


IN_TN = 512
IN_NR = RWKV_PCOLS // IN_TN
IN_NS = SB_PCOLS // IN_TN


def _ln_inproj_kernel(x_ref, g_ref, b_ref, w_ref, pr_ref, ps_ref, hb_ref):
    j = pl.program_id(1)

    @pl.when(j == 0)
    def _():
        hb_ref[...] = _layer_norm(x_ref[...], g_ref[...], b_ref[...]).astype(BF16)

    acc = _dot(hb_ref[...], w_ref[...])

    @pl.when(j < IN_NR)
    def _():
        pr_ref[...] = acc

    @pl.when(j >= IN_NR)
    def _():
        ps_ref[...] = acc.astype(BF16)


def _ln_inproj(x, g, b, w, tm):
    n = x.shape[0]
    return pl.pallas_call(
        _ln_inproj_kernel,
        grid=(n // tm, IN_NR + IN_NS),
        in_specs=[
            pl.BlockSpec((tm, D_MODEL), lambda i, j: (i, 0)),
            pl.BlockSpec((1, D_MODEL), lambda i, j: (0, 0)),
            pl.BlockSpec((1, D_MODEL), lambda i, j: (0, 0)),
            pl.BlockSpec((D_MODEL, IN_TN), lambda i, j: (0, j)),
        ],
        out_specs=[
            pl.BlockSpec((tm, IN_TN), lambda i, j: (i, jnp.minimum(j, IN_NR - 1))),
            pl.BlockSpec((tm, IN_TN), lambda i, j: (i, jnp.maximum(j - IN_NR, 0))),
        ],
        out_shape=[
            jax.ShapeDtypeStruct((n, RWKV_PCOLS), F32),
            jax.ShapeDtypeStruct((n, SB_PCOLS), BF16),
        ],
        scratch_shapes=[pltpu.VMEM((tm, D_MODEL), BF16)],
        compiler_params=_cparams(("arbitrary", "arbitrary")),
        name="ln_inproj",
    )(x, g, b, w)


SB_TQ = 256
SB_W = MXU_DIM
SB_HEADS = SB_W // HEAD_DIM
SB_GROUPS_PER_STEP = 2


def _sb_setup(carry_ref, acc_ref, tq):
    r = lax.broadcasted_iota(jnp.int32, (2 * SB_BLOCK, 2 * SB_BLOCK), 0) & (SB_BLOCK - 1)
    c = lax.broadcasted_iota(jnp.int32, (2 * SB_BLOCK, 2 * SB_BLOCK), 1)
    u = jnp.where((c >= SB_BLOCK) | (r >= c), -1.0, 0.0).astype(BF16)
    carry_ref[...] = jnp.zeros_like(carry_ref)
    acc_ref[...] = jnp.zeros_like(acc_ref)
    row = lax.broadcasted_iota(jnp.int32, (tq, SB_BLOCK), 0)
    col = lax.broadcasted_iota(jnp.int32, (tq, SB_BLOCK), 1)
    return u, row, col


def _sb_block_diag(x):
    rb = lax.broadcasted_iota(jnp.int32, (SB_HEADS * SB_BLOCK, SB_W), 0)
    cb = lax.broadcasted_iota(jnp.int32, (SB_HEADS * SB_BLOCK, SB_W), 1)
    bd_mask = (rb >> 7) == (cb >> LOG2_HEAD)
    return jnp.where(bd_mask, jnp.concatenate([x] * SB_HEADS, axis=0), jnp.zeros((), BF16))


def _sb_weights(z_all, mask, u, carry_ref, rows=slice(None)):
    z = [za[:, h * SB_BLOCK:(h + 1) * SB_BLOCK] for za in z_all for h in range(SB_HEADS)]
    m = _each(lambda a: jnp.maximum(a, 0.0) + jnp.log(1.0 + jnp.exp(_neg_abs(a))), z)
    if mask is not None:
        m = _each(lambda a: jnp.where(mask, a, 0.0), m)
    hl = _each(lambda a: jnp.concatenate(_split(a), axis=1), m)
    at = _each(lambda a: _dot(a, u), hl)
    carry = [carry_ref[h, rows, :] for h in range(len(z))]
    w = _each(lambda a, b, c: jnp.exp(a + b[:, :SB_BLOCK] + c), z, at, carry)
    if mask is not None:
        w = _each(lambda a: jnp.where(mask, a, 0.0), w)
    for h in range(len(z)):
        carry_ref[h, rows, :] = carry[h] + at[h][:, SB_BLOCK:]
    w = [a.astype(BF16) for a in w]
    return [jnp.concatenate(w[g * SB_HEADS:(g + 1) * SB_HEADS], axis=1) for g in range(len(z_all))]


def _sb_step(q, kb, vb, mask, u, carry_ref, acc_ref):
    w, = _sb_weights([_dot_nt(q, _sb_block_diag(kb))], mask, u, carry_ref)
    acc_ref[...] += _dot(w, _sb_block_diag(vb))


def _sb_finish(acc_ref, g_ref, o_ref):
    r = lax.broadcasted_iota(jnp.int32, (2 * SB_W, SB_W), 0) & (SB_W - 1)
    c = lax.broadcasted_iota(jnp.int32, (2 * SB_W, SB_W), 1)
    seg_ones = ((r >> LOG2_HEAD) == (c >> LOG2_HEAD)).astype(BF16)
    for g in range(acc_ref.shape[1] // SB_W):
        gs = slice(g * SB_W, (g + 1) * SB_W)
        acc = acc_ref[:, gs]
        ms = _dot(jnp.concatenate(_split(acc * acc), axis=1), seg_ones) * (1.0 / HEAD_DIM)
        o_ref[:, gs] = (acc * lax.rsqrt(ms + RMS_EPS) * g_ref[:, gs]).astype(o_ref.dtype)


def _sb_kernel(q_ref, k_ref, v_ref, km_ref, vm_ref, g_ref, o_ref, carry_ref, acc_ref,
               kbd_ref, vbd_ref, z0_ref, z1_ref, w0_ref, w1_ref, *, tq, n_blocks):
    qi = pl.program_id(2)
    assert tq == 2 * SB_BLOCK
    u, row, col = _sb_setup(carry_ref, acc_ref, tq)
    q = q_ref[...]
    groups = range(q.shape[1] // SB_W)
    gsl = [slice(g * SB_W, (g + 1) * SB_W) for g in groups]
    every = slice(None)

    @pl.when(qi == 0)
    def _():
        def fill(j, carry):
            s = pl.multiple_of(j * SB_BLOCK, SB_BLOCK)
            for g in groups:
                kbd_ref[g, j] = _sb_block_diag(k_ref[pl.ds(s, SB_BLOCK), gsl[g]])
                vbd_ref[g, j] = _sb_block_diag(v_ref[pl.ds(s, SB_BLOCK), gsl[g]])
            return carry
        lax.fori_loop(0, n_blocks, fill, 0)
        for g in groups:
            kbd_ref[g, n_blocks] = _sb_block_diag(km_ref[:, gsl[g]])
            vbd_ref[g, n_blocks] = _sb_block_diag(vm_ref[:, gsl[g]])

    def logits(blk, z_ref, rows=every):
        for g in groups:
            z_ref[g, rows, :] = _dot_nt(q[rows, gsl[g]], kbd_ref[g, blk])

    def weights(z_ref, w_ref, mask, rows=every):
        ws = _sb_weights([z_ref[g, rows, :] for g in groups], mask, u, carry_ref, rows)
        for g in groups:
            w_ref[g, rows, :] = ws[g]

    def pv(blk, w_ref, rows=every):
        for g in groups:
            acc_ref[rows, gsl[g]] += _dot(w_ref[g, rows, :], vbd_ref[g, blk])

    top = 2 * qi + 1
    low = slice(SB_BLOCK, tq)
    logits(top, z0_ref, low)
    logits(top - 1, z1_ref)
    weights(z0_ref, w0_ref, (col < row)[:SB_BLOCK], low)
    pv(top, w0_ref, low)
    logits(jnp.where(qi == 0, n_blocks, top - 2), z0_ref)
    weights(z1_ref, w1_ref, col < row)

    def body(m, carry):
        b0 = top - 2 - 2 * m
        logits(b0 - 1, z1_ref)
        pv(b0 + 1, w1_ref)
        weights(z0_ref, w0_ref, None)
        logits(jnp.where(m == qi - 1, n_blocks, b0 - 2), z0_ref)
        pv(b0, w0_ref)
        weights(z1_ref, w1_ref, None)
        return carry

    lax.fori_loop(0, qi, body, 0)
    pv(0, w1_ref)
    weights(z0_ref, w0_ref, col < N_META)
    pv(n_blocks, w0_ref)
    _sb_finish(acc_ref, g_ref, o_ref)


def _sb_meta_kernel(q_ref, k_ref, v_ref, g_ref, o_ref, carry_ref, acc_ref):
    u, row, col = _sb_setup(carry_ref, acc_ref, META_PAD)
    _sb_step(q_ref[...], k_ref[...], v_ref[...], col < row, u, carry_ref, acc_ref)
    _sb_finish(acc_ref, g_ref, o_ref)


def _sb_scratch(tq, n_groups=1):
    return [pltpu.VMEM((n_groups * SB_HEADS, tq, SB_BLOCK), F32), pltpu.VMEM((tq, n_groups * SB_W), F32)]


def _sb_attention(ps, ps_meta, norm_g, batch, seq):
    ng = SB_GROUPS_PER_STEP
    w = ng * SB_W
    n_g = SB_WIDTH // w
    nq = seq // SB_TQ
    n_blocks = seq // SB_BLOCK
    kern = functools.partial(_sb_kernel, tq=SB_TQ, n_blocks=n_blocks)
    bd_rows = SB_HEADS * SB_BLOCK
    pipeline_scratch = [
        pltpu.VMEM((ng, n_blocks + 1, bd_rows, SB_W), BF16),
        pltpu.VMEM((ng, n_blocks + 1, bd_rows, SB_W), BF16),
        pltpu.VMEM((ng, SB_TQ, bd_rows), F32), pltpu.VMEM((ng, SB_TQ, bd_rows), F32),
        pltpu.VMEM((ng, SB_TQ, bd_rows), BF16), pltpu.VMEM((ng, SB_TQ, bd_rows), BF16),
    ]
    return pl.pallas_call(
        kern,
        grid=(batch, n_g, nq),
        in_specs=[
            pl.BlockSpec((SB_TQ, w), lambda b, h, q: (b * nq + q, h)),
            pl.BlockSpec((seq, w), lambda b, h, q: (b, n_g + h)),
            pl.BlockSpec((seq, w), lambda b, h, q: (b, 2 * n_g + h)),
            pl.BlockSpec((META_PAD, w), lambda b, h, q: (0, n_g + h)),
            pl.BlockSpec((META_PAD, w), lambda b, h, q: (0, 2 * n_g + h)),
            pl.BlockSpec((1, w), lambda b, h, q: (0, h)),
        ],
        out_specs=pl.BlockSpec((SB_TQ, w), lambda b, h, q: (b * nq + q, h)),
        out_shape=jax.ShapeDtypeStruct((batch * seq, SB_WIDTH), BF16),
        scratch_shapes=_sb_scratch(SB_TQ, ng) + pipeline_scratch,
        compiler_params=_cparams(("arbitrary", "arbitrary", "arbitrary")),
        name="sb_attention",
    )(ps, ps, ps, ps_meta, ps_meta, norm_g)


def _sb_attention_meta(ps_meta, norm_g):
    w = SB_W
    n_g = SB_WIDTH // w
    return pl.pallas_call(
        _sb_meta_kernel,
        grid=(n_g,),
        in_specs=[
            pl.BlockSpec((META_PAD, w), lambda h: (0, h)),
            pl.BlockSpec((META_PAD, w), lambda h: (0, n_g + h)),
            pl.BlockSpec((META_PAD, w), lambda h: (0, 2 * n_g + h)),
            pl.BlockSpec((1, w), lambda h: (0, h)),
        ],
        out_specs=pl.BlockSpec((META_PAD, w), lambda h: (0, h)),
        out_shape=jax.ShapeDtypeStruct((META_PAD, SB_WIDTH), BF16),
        scratch_shapes=_sb_scratch(META_PAD),
        compiler_params=_cparams(("arbitrary",)),
        name="sb_attention_meta",
    )(ps_meta, ps_meta, ps_meta, norm_g)


def _bd(x, seg_ones):
    return jnp.concatenate([x.astype(BF16)] * HEADS_PER_GROUP, axis=0) * seg_ones


def _seg_sum(a, seg_ones):
    return _dot(a.astype(BF16), seg_ones)


def _mm_bd(a, b, seg_ones):
    return _dot(a.astype(BF16), _bd(b, seg_ones))


def _rwkv_kernel(pm_ref, pl_ref, prev_m_ref, prev_l_ref, s0_ref, mu_m_ref, mu_l_ref,
                 w0_ref, w2_ref, a0_ref, a2_ref, g2_ref, kk_ref, ka_ref, rk_ref, gng_ref, gnb_ref,
                 y_ref, s_out_ref, s_ref, carry_m_ref, carry_l_ref, *, n_valid):
    c = pl.program_id(1)
    L = CHUNK
    nb = pm_ref.shape[0]

    @pl.when(c == 0)
    def _():
        for bi in range(nb):
            s_ref[bi * N_GROUPS:(bi + 1) * N_GROUPS] = s0_ref[...]
            carry_m_ref[bi] = prev_m_ref[...]
            carry_l_ref[bi] = prev_l_ref[...]

    row = lax.broadcasted_iota(jnp.int32, (L, 1), 0)
    first = row == 0

    def token_shift(p, prev_last, mu):
        shifted = jnp.where(first, prev_last, pltpu.roll(p, 1, 0))
        return p + (shifted - p) * mu

    r4 = lax.broadcasted_iota(jnp.int32, (4 * L, GROUP), 0)
    c4 = lax.broadcasted_iota(jnp.int32, (4 * L, GROUP), 1)
    bd_mask = (r4 >> LOG2_CHUNK) == (c4 >> LOG2_HEAD)
    seg_ones = bd_mask.astype(BF16)
    ri = lax.broadcasted_iota(jnp.int32, (L, L), 0)
    ci = lax.broadcasted_iota(jnp.int32, (L, L), 1)
    tri_incl = (ci <= ri).astype(BF16)
    r2 = lax.broadcasted_iota(jnp.int32, (2 * L, 4 * L), 0)
    c2 = lax.broadcasted_iota(jnp.int32, (2 * L, 4 * L), 1) & (L - 1)
    low_mask = c2 < (r2 & (L - 1)) + (r2 >> LOG2_CHUNK)
    rl = lax.broadcasted_iota(jnp.int32, (L, 4 * L), 0)
    cl = lax.broadcasted_iota(jnp.int32, (L, 4 * L), 1) & (L - 1)
    eye_t = (rl == cl).astype(F32)
    consts = (row, first, bd_mask, seg_ones, tri_incl, low_mask, eye_t)

    gsl = [slice(g * GROUP, (g + 1) * GROUP) for g in range(N_GROUPS)]
    xm, lora = [], ([], [], [])
    for bi in range(nb):
        pl_blk = pl_ref[bi]
        xl = token_shift(pl_blk, carry_l_ref[bi, 7:8, :], mu_l_ref[...])
        carry_l_ref[bi] = pl_blk[L - 8:, :]
        parts = (jnp.tanh(xl[:, 0:LORA_W]).astype(BF16), xl[:, LORA_W:2 * LORA_W].astype(BF16),
                 _sigmoid(xl[:, 2 * LORA_W:]).astype(BF16))
        pm_blk = pm_ref[bi]
        xm_all = token_shift(pm_blk, carry_m_ref[bi, 7:8, :], mu_m_ref[...])
        carry_m_ref[bi] = pm_blk[L - 8:, :]
        xm += [xm_all[:, g * 3 * GROUP:(g + 1) * 3 * GROUP] for g in range(N_GROUPS)]
        for dst, part in zip(lora, parts):
            dst += [part] * N_GROUPS
    prm = [[ref[:, gs] for gs in gsl] for ref in
           (w0_ref, w2_ref, a0_ref, a2_ref, g2_ref, kk_ref, ka_ref, rk_ref, gng_ref, gnb_ref)]
    n_chains = nb * N_GROUPS
    ys, s_new = _rwkv_groups(xm, lora, [p * nb for p in prm], [s_ref[i] for i in range(n_chains)],
                             consts, n_valid)
    for i in range(n_chains):
        s_ref[i] = s_new[i]
        s_out_ref[i] = s_new[i]
        y_ref[i // N_GROUPS, :, gsl[i % N_GROUPS]] = ys[i].astype(y_ref.dtype)


def _rwkv_groups(xm, lora, prm, s_t, consts, n_valid):
    L = CHUNK
    row, first, bd_mask, seg_ones, tri_incl, low_mask, eye_t = consts
    w0, w2, a0, a2, g2, k_k, k_a, r_k, gn_g, gn_b = prm
    tanh_wl, al, sig_gl = lora
    r = [x[:, 0:GROUP] for x in xm]
    k = [x[:, GROUP:2 * GROUP] for x in xm]
    v = [x[:, 2 * GROUP:3 * GROUP] for x in xm]

    dw = _each(lambda b, x, w: b + _dot(x, w), w0, tanh_wl, w2)
    w_log = _each(lambda d: jnp.minimum(d, 0.0) - jnp.log(1.0 + jnp.exp(_neg_abs(d))) - 0.5, dw)
    logw = _each(lambda x: -jnp.exp(x), w_log)
    iclr = _each(lambda b, x, w: _sigmoid(b + _dot(x, w)), a0, al, a2)
    gate = _each(_dot, sig_gl, g2)

    kk = _each(lambda a, b: a * b, k, k_k)
    kk = _each(lambda x: x * lax.rsqrt(jnp.maximum(_seg_sum(x * x, seg_ones), 1e-24)), kk)
    kmod = _each(lambda x, i, a: x * (1.0 + (i - 1.0) * a), k, iclr, k_a)

    if n_valid < L:
        valid = row < n_valid
        zero_pad = lambda x: jnp.where(valid, x, 0.0)
        kk, kmod, v, logw = _each(zero_pad, kk), _each(zero_pad, kmod), _each(zero_pad, v), _each(zero_pad, logw)

    cum = _each(lambda x: _mm_hi_lhs_const(tri_incl, x), logw)
    p_in = _each(jnp.exp, cum)
    p_ex = _each(lambda c, w: jnp.exp(c - w), cum, logw)
    p_inv = _each(lambda c: jnp.exp(-c), cum)
    p_tail = _each(lambda c: jnp.exp(c[L - 1:L, :] - c), cum)
    p_last = _each(lambda c: jnp.exp(c[L - 1:L, :]), cum)
    at = _each(lambda a, p: -a * p, kk, p_ex)
    kb = _each(lambda a, i: a * i, kk, iclr)
    bt = _each(lambda a, p: (a * p).astype(BF16), kb, p_inv)
    kt = _each(lambda a, p: (a * p).astype(BF16), kmod, p_inv)
    rt = _each(lambda a, p: a * p, r, p_in)

    ar = _each(lambda a, b: jnp.concatenate([a, b], axis=0).astype(BF16), at, rt)
    abr = _each(lambda a, b: jnp.where(low_mask, _dot_nt(a, _bd(b, seg_ones)), 0.0), ar, bt)
    akr = _each(lambda a, b: jnp.where(low_mask, _dot_nt(a, _bd(b, seg_ones)), 0.0), ar, kt)
    sa = _each(lambda a, s: _dot_nt(a, s.astype(BF16)), ar, s_t)
    bd_v = _each(lambda x: _bd(x, seg_ones), v)
    x = _each(lambda s, a, b: s[:L] + _dot(a[:L].astype(BF16), b), sa, akr, bd_v)

    acc = [eye_t] * len(xm)
    pw = [a[:L] for a in abr]
    for lvl in range(LOG2_CHUNK):
        if lvl < LOG2_CHUNK - 1:
            res = _each(lambda a, p: _mm_bd(jnp.concatenate([a, p], axis=0), p, seg_ones), acc, pw)
            acc = _each(lambda a, rr: a + rr[:L], acc, res)
            pw = [rr[L:] for rr in res]
        else:
            acc = _each(lambda a, p: a + _mm_bd(a, p, seg_ones), acc, pw)
    u = _each(lambda a, b: _mm_bd(a, b, seg_ones), acc, x)

    rbk = _each(lambda a, b: jnp.concatenate([a[L:], b[L:]], axis=1).astype(BF16), abr, akr)
    uv_bd = _each(lambda a, b: jnp.concatenate([_bd(a, seg_ones), b], axis=0), u, bd_v)
    y = _each(lambda s, a, b: s[L:] + _dot(a, b), sa, rbk, uv_bd)

    uv_t = _each(lambda a, b: jnp.transpose(jnp.concatenate([a, b], axis=0)).astype(BF16), u, v)
    bk_tail = _each(lambda a, b, p: jnp.concatenate([a * p, b * p], axis=0).astype(BF16), kb, kmod, p_tail)
    s_new = _each(lambda s, p, a, b: s * p + jnp.where(bd_mask, _dot(a, b), 0.0), s_t, p_last, uv_t, bk_tail)

    mean = _each(lambda a: _seg_sum(a, seg_ones) * (1.0 / HEAD_DIM), y)
    yc = _each(lambda a, m: a - m, y, mean)
    var = _each(lambda a: _seg_sum(a * a, seg_ones) * (1.0 / HEAD_DIM), yc)
    yn = _each(lambda a, s, g_, b_: a * lax.rsqrt(s + GN_EPS) * g_ + b_, yc, var, gn_g, gn_b)
    bonus = _each(lambda a, b, c, d: _seg_sum(a * b * c, seg_ones) * d, r, kmod, r_k, v)
    out = _each(lambda a, b, c: (a + b) * c, yn, bonus, gate)
    return out, s_new


def _mm_hi_lhs_const(a_bf16, b):
    hi, lo = _split(b)
    return _dot(a_bf16, hi) + _dot(a_bf16, lo)


def _rwkv(p, prev, prev_blk, s0, prm, nb, n_valid):
    batch, rows, _ = p.shape
    n_chunks = rows // CHUNK
    n_main = 3 * RWKV_WIDTH
    lora_blk = n_main // LORA_COLS
    row_map = lambda b, c: (b, c, 0)
    whole = lambda b, c: (0, 0)
    vec = pl.BlockSpec((1, RWKV_WIDTH), whole)
    kern = functools.partial(_rwkv_kernel, n_valid=n_valid)
    return pl.pallas_call(
        kern,
        grid=(batch // nb, n_chunks),
        in_specs=[
            pl.BlockSpec((nb, CHUNK, n_main), row_map),
            pl.BlockSpec((nb, CHUNK, LORA_COLS), lambda b, c: (b, c, lora_blk)),
            pl.BlockSpec((8, n_main), lambda b, c: (prev_blk, 0)),
            pl.BlockSpec((8, LORA_COLS), lambda b, c: (prev_blk, lora_blk)),
            pl.BlockSpec((N_GROUPS, GROUP, GROUP), lambda b, c: (0, 0, 0)),
            pl.BlockSpec((1, n_main), whole),
            pl.BlockSpec((1, LORA_COLS), whole),
            vec,
            pl.BlockSpec((LORA_W, RWKV_WIDTH), whole),
            vec,
            pl.BlockSpec((LORA_W, RWKV_WIDTH), whole),
            pl.BlockSpec((LORA_G, RWKV_WIDTH), whole),
            vec, vec, vec, vec, vec,
        ],
        out_specs=[
            pl.BlockSpec((nb, CHUNK, RWKV_WIDTH), row_map),
            pl.BlockSpec((nb * N_GROUPS, GROUP, GROUP), lambda b, c: (b, 0, 0)),
        ],
        out_shape=[
            jax.ShapeDtypeStruct((batch, rows, RWKV_WIDTH), BF16),
            jax.ShapeDtypeStruct((batch * N_GROUPS, GROUP, GROUP), F32),
        ],
        scratch_shapes=[
            pltpu.VMEM((nb * N_GROUPS, GROUP, GROUP), F32),
            pltpu.VMEM((nb, 8, n_main), F32),
            pltpu.VMEM((nb, 8, LORA_COLS), F32),
        ],
        compiler_params=_cparams(("arbitrary", "arbitrary")),
        name="rwkv7",
    )(p, p, prev, prev, s0, prm["mu_m"], prm["mu_l"], prm["w0"], prm["w2"], prm["a0"], prm["a2"],
      prm["g2"], prm["k_k"], prm["k_a"], prm["r_k"], prm["gn_g"], prm["gn_b"])


def _outproj_ln_kernel(yr_ref, ys_ref, x_ref, w_ref, g0_ref, b0_ref, g_ref, b_ref, o_ref):
    h = _layer_norm(x_ref[...], g0_ref[...], b0_ref[...])
    mix = _dot(yr_ref[...], w_ref[0:RWKV_WIDTH, :]) + _dot(ys_ref[...], w_ref[RWKV_WIDTH:, :])
    o_ref[...] = _layer_norm(DEEPNORM_ALPHA * h + mix, g_ref[...], b_ref[...])


def _outproj_ln(yr, ys, x, w, g0, b0, g, b, tm):
    n = x.shape[0]
    return pl.pallas_call(
        _outproj_ln_kernel,
        grid=(n // tm,),
        in_specs=[
            pl.BlockSpec((tm, RWKV_WIDTH), lambda i: (i, 0)),
            pl.BlockSpec((tm, SB_WIDTH), lambda i: (i, 0)),
            pl.BlockSpec((tm, D_MODEL), lambda i: (i, 0)),
            pl.BlockSpec((D_MODEL, D_MODEL), lambda i: (0, 0)),
            pl.BlockSpec((1, D_MODEL), lambda i: (0, 0)),
            pl.BlockSpec((1, D_MODEL), lambda i: (0, 0)),
            pl.BlockSpec((1, D_MODEL), lambda i: (0, 0)),
            pl.BlockSpec((1, D_MODEL), lambda i: (0, 0)),
        ],
        out_specs=pl.BlockSpec((tm, D_MODEL), lambda i: (i, 0)),
        out_shape=jax.ShapeDtypeStruct((n, D_MODEL), F32),
        compiler_params=_cparams(("arbitrary",)),
        name="outproj_ln",
    )(yr, ys, x, w, g0, b0, g, b)


FF_TF = 512
FF_NF = D_FF // FF_TF


def _gate_halo_kernel(h_ref, w_ref, o_ref):
    o_ref[0] = _dot(h_ref[...].astype(BF16), w_ref[...])


def _gate_halo(h_rows, w_up):
    return pl.pallas_call(
        _gate_halo_kernel,
        grid=(FF_NF,),
        in_specs=[
            pl.BlockSpec((8, D_MODEL), lambda f: (N_META // 8 - 1, 0)),
            pl.BlockSpec((D_MODEL, FF_TF), lambda f: (0, f)),
        ],
        out_specs=pl.BlockSpec((1, 8, FF_TF), lambda f: (f, 0, 0)),
        out_shape=jax.ShapeDtypeStruct((FF_NF, 8, FF_TF), F32),
        compiler_params=_cparams(("arbitrary",)),
        name="ffn_gate_halo",
    )(h_rows, w_up)


def _ffn_kernel(h_ref, halo0_ref, wg_ref, wv_ref, cw_ref, cb_ref, wd_ref, g_ref, b_ref, o_ref,
                hb_ref, acc_ref, halo_ref, *, tiles_per_seq):
    i = pl.program_id(0)
    f = pl.program_id(1)
    tm = h_ref.shape[0]

    @pl.when(f == 0)
    def _():
        hb_ref[...] = h_ref[...].astype(BF16)
        acc_ref[...] = jnp.zeros_like(acc_ref)

    @pl.when((i % tiles_per_seq) == 0)
    def _():
        halo_ref[f] = halo0_ref[0]

    hb = hb_ref[...]
    row = lax.broadcasted_iota(jnp.int32, (tm, 1), 0)
    halves = [slice(s * MXU_DIM, (s + 1) * MXU_DIM) for s in range(FF_TF // MXU_DIM)]
    gate = [_dot(hb, wg_ref[:, cs]) for cs in halves]
    val = [_dot(hb, wv_ref[:, cs]) for cs in halves]
    down = []
    for cs, g, v in zip(halves, gate, val):
        halo = halo_ref[f, :, cs]
        halo_ref[f, :, cs] = g[tm - 8:, :]
        g1 = jnp.where(row == 0, halo[7:8, :], pltpu.roll(g, 1, 0))
        g2 = jnp.where(row == 0, halo[6:7, :], jnp.where(row == 1, halo[7:8, :], pltpu.roll(g, 2, 0)))
        conv = cw_ref[0:1, cs] * g2 + cw_ref[1:2, cs] * g1 + cw_ref[2:3, cs] * g + cb_ref[:, cs]
        act = (conv * _sigmoid(conv) * v).astype(BF16)
        down.append(_dot(act, wd_ref[cs, :]))
    acc_ref[...] += functools.reduce(lambda a, b: a + b, down)

    @pl.when(f == FF_NF - 1)
    def _():
        o_ref[...] = _layer_norm(DEEPNORM_ALPHA * h_ref[...] + acc_ref[...], g_ref[...], b_ref[...])


def _ffn(h, halo0, w_up, conv_w, conv_b, w_down, g, b, tm, seq):
    n = h.shape[0]
    kern = functools.partial(_ffn_kernel, tiles_per_seq=seq // tm)
    return pl.pallas_call(
        kern,
        grid=(n // tm, FF_NF),
        in_specs=[
            pl.BlockSpec((tm, D_MODEL), lambda i, f: (i, 0)),
            pl.BlockSpec((1, 8, FF_TF), lambda i, f: (f, 0, 0)),
            pl.BlockSpec((D_MODEL, FF_TF), lambda i, f: (0, f)),
            pl.BlockSpec((D_MODEL, FF_TF), lambda i, f: (0, FF_NF + f)),
            pl.BlockSpec((3, FF_TF), lambda i, f: (0, f)),
            pl.BlockSpec((1, FF_TF), lambda i, f: (0, f)),
            pl.BlockSpec((FF_TF, D_MODEL), lambda i, f: (f, 0)),
            pl.BlockSpec((1, D_MODEL), lambda i, f: (0, 0)),
            pl.BlockSpec((1, D_MODEL), lambda i, f: (0, 0)),
        ],
        out_specs=pl.BlockSpec((tm, D_MODEL), lambda i, f: (i, 0)),
        out_shape=jax.ShapeDtypeStruct((n, D_MODEL), F32),
        scratch_shapes=[
            pltpu.VMEM((tm, D_MODEL), BF16),
            pltpu.VMEM((tm, D_MODEL), F32),
            pltpu.VMEM((FF_NF, 8, FF_TF), F32),
        ],
        compiler_params=_cparams(("arbitrary", "arbitrary")),
        name="conv_ffn",
    )(h, halo0, w_up, w_up, conv_w, conv_b, w_down, g, b)


def _group_major(a):
    parts = [a[..., s * RWKV_WIDTH:(s + 1) * RWKV_WIDTH] for s in range(3)]
    out = []
    for gi in range(N_GROUPS):
        out += [p[..., gi * GROUP:(gi + 1) * GROUP] for p in parts]
    return jnp.concatenate(out, axis=-1)


def _pad_cols(a, width):
    return jnp.pad(a, [(0, 0)] * (a.ndim - 1) + [(0, width - a.shape[-1])])


def _pad_rows(a, rows):
    return jnp.pad(a, [(0, rows - a.shape[0])] + [(0, 0)] * (a.ndim - 1))


def _lora_slots(a):
    c = 3 * RWKV_WIDTH
    wl = a[..., c:c + DECAY_LORA]
    al = a[..., c + DECAY_LORA:c + DECAY_LORA + ICLR_LORA]
    gl = a[..., c + DECAY_LORA + ICLR_LORA:c + DECAY_LORA + ICLR_LORA + GATE_LORA]
    return jnp.concatenate([_pad_cols(wl, LORA_W), _pad_cols(al, LORA_W), _pad_cols(gl, LORA_G)], axis=-1)


def kernel(x, meta_tokens, emb_ln_g, emb_ln_b, w_in, rwkv_mu, rwkv_w0, rwkv_w2, rwkv_a0, rwkv_a2, rwkv_g2, rwkv_k_k, rwkv_k_a, rwkv_r_k, rwkv_gn_g, rwkv_gn_b, sb_norm_g, w_out, ln1_g, ln1_b, ffn_w_up, ffn_conv_w, ffn_conv_b, ffn_w_down, ln2_g, ln2_b):
    batch, seq, d = x.shape
    assert d == D_MODEL and seq % 512 == 0 and w_in.shape[0] == 1 and batch % RWKV_BATCH_PER_STEP == 0
    n_rw = 3 * RWKV_WIDTH + DECAY_LORA + ICLR_LORA + GATE_LORA
    row2 = lambda a: a.reshape(1, -1)
    grp3 = lambda a: a.reshape(N_GROUPS, 1, GROUP)

    wi = w_in[0].astype(BF16)
    sb_cols = wi[:, n_rw:]
    sb_cols = jnp.concatenate([sb_cols[:, :SB_WIDTH] * (HEAD_DIM ** -0.5), sb_cols[:, SB_WIDTH:]], axis=1)
    w1 = jnp.concatenate([_group_major(wi[:, :3 * RWKV_WIDTH]), _lora_slots(wi), sb_cols], axis=1)
    mu = rwkv_mu[0]
    prm = {
        "mu_m": row2(_group_major(mu[:3 * RWKV_WIDTH])),
        "mu_l": row2(_lora_slots(mu)),
        "w0": row2(rwkv_w0[0]),
        "w2": _pad_rows(rwkv_w2[0], LORA_W).astype(BF16),
        "a0": row2(rwkv_a0[0]),
        "a2": _pad_rows(rwkv_a2[0], LORA_W).astype(BF16),
        "g2": _pad_rows(rwkv_g2[0], LORA_G).astype(BF16),
        "k_k": row2(rwkv_k_k[0]),
        "k_a": row2(rwkv_k_a[0]),
        "r_k": row2(rwkv_r_k[0]),
        "gn_g": row2(rwkv_gn_g[0]),
        "gn_b": row2(rwkv_gn_b[0]),
    }
    w_o = w_out[0].astype(BF16)
    w_up = ffn_w_up[0].astype(BF16)
    w_dn = ffn_w_down[0].astype(BF16)
    g0, b0 = row2(emb_ln_g), row2(emb_ln_b)
    g1, b1 = row2(ln1_g[0]), row2(ln1_b[0])
    g2, b2 = row2(ln2_g[0]), row2(ln2_b[0])
    sbg = row2(sb_norm_g[0])

    xm = _pad_rows(meta_tokens.astype(x.dtype), META_PAD)
    pr_m, ps_m = _ln_inproj(xm, g0, b0, w1, META_PAD)
    ysb_m = _sb_attention_meta(ps_m, sbg)
    zeros_state = jnp.zeros((N_GROUPS, GROUP, GROUP), F32)
    yr_m, s_meta = _rwkv(pr_m[:CHUNK].reshape(1, CHUNK, RWKV_PCOLS), jnp.zeros((8, RWKV_PCOLS), F32), 0,
                         zeros_state, prm, 1, N_META)
    h1_m = _outproj_ln(_pad_rows(yr_m[0], META_PAD), ysb_m, xm, w_o, g0, b0, g1, b1, META_PAD)
    halo0 = _gate_halo(h1_m, w_up)

    xr = x.reshape(batch * seq, d)
    pr, ps = _ln_inproj(xr, g0, b0, w1, 1024)
    y_sb = _sb_attention(ps, ps_m, sbg, batch, seq)
    y_rw, _ = _rwkv(pr.reshape(batch, seq, RWKV_PCOLS), pr_m, N_META // 8 - 1, s_meta, prm,
                    RWKV_BATCH_PER_STEP, CHUNK)
    h1 = _outproj_ln(y_rw.reshape(batch * seq, RWKV_WIDTH), y_sb, xr, w_o, g0, b0, g1, b1, 512)
    out = _ffn(h1, halo0, w_up, ffn_conv_w[0], row2(ffn_conv_b[0]), w_dn, g2, b2, 512, seq)
    return out.reshape(batch, seq, d)
```

```python
import functools

import jax
import jax.numpy as jnp
from jax import lax
from jax.experimental import pallas as pl
from jax.experimental.pallas import tpu as pltpu

F32 = jnp.float32
BF16 = jnp.bfloat16

D_MODEL = 2048
N_META = 16
HEAD_DIM = 64
RWKV_WIDTH = 1024
SB_WIDTH = 1024
DECAY_LORA = 64
ICLR_LORA = 64
GATE_LORA = 160
D_FF = 5632
DEPTH = 1
DEEPNORM_ALPHA = (2.0 * DEPTH) ** 0.25
LN_EPS = 1e-5
GN_EPS = 64e-5
RMS_EPS = 1e-6

LANES = 128
MXU_DIM = 256

GROUP = MXU_DIM
N_GROUPS = RWKV_WIDTH // GROUP
HEADS_PER_GROUP = GROUP // HEAD_DIM
CHUNK = 64
LOG2_CHUNK = 6
LOG2_HEAD = 6
RWKV_BATCH_PER_STEP = 4
LORA_W = 128
LORA_G = 256
LORA_COLS = 2 * LORA_W + LORA_G
RWKV_PCOLS = 3 * RWKV_WIDTH + LORA_COLS
SB_PCOLS = 3 * SB_WIDTH
SB_BLOCK = 128
META_PAD = 128

VMEM_LIMIT = 56 * 1024 * 1024


def _cparams(sem, flags=None):
    return pltpu.CompilerParams(dimension_semantics=sem, vmem_limit_bytes=VMEM_LIMIT, flags=flags)


def _layer_norm(x, g, b):
    mu = jnp.mean(x, axis=-1, keepdims=True)
    xc = x - mu
    var = jnp.mean(xc * xc, axis=-1, keepdims=True)
    return xc * lax.rsqrt(var + LN_EPS) * g + b


def _dot(a, b):
    return jnp.dot(a, b, preferred_element_type=F32)


def _dot_nt(a, b):
    return lax.dot_general(a, b, (((1,), (1,)), ((), ())), preferred_element_type=F32)


def _split(x):
    hi = x.astype(BF16)
    lo = (x - hi.astype(F32)).astype(BF16)
    return hi, lo


def _neg_abs(x):
    bits = lax.bitcast_convert_type(x, jnp.uint32) | jnp.uint32(0x80000000)
    return lax.bitcast_convert_type(bits, F32)


def _sigmoid(x):
    return 1.0 / (1.0 + jnp.exp(-x))


def _each(f, *lists):
    return [f(*args) for args in zip(*lists)]


IN_TN = 512
IN_NR = RWKV_PCOLS // IN_TN
IN_NS = SB_PCOLS // IN_TN


def _ln_inproj_kernel(x_ref, g_ref, b_ref, w_ref, pr_ref, ps_ref, hb_ref):
    j = pl.program_id(1)

    @pl.when(j == 0)
    def _():
        hb_ref[...] = _layer_norm(x_ref[...], g_ref[...], b_ref[...]).astype(BF16)

    acc = _dot(hb_ref[...], w_ref[...])

    @pl.when(j < IN_NR)
    def _():
        pr_ref[...] = acc

    @pl.when(j >= IN_NR)
    def _():
        ps_ref[...] = acc.astype(BF16)


def _ln_inproj(x, g, b, w, tm):
    n = x.shape[0]
    return pl.pallas_call(
        _ln_inproj_kernel,
        grid=(n // tm, IN_NR + IN_NS),
        in_specs=[
            pl.BlockSpec((tm, D_MODEL), lambda i, j: (i, 0)),
            pl.BlockSpec((1, D_MODEL), lambda i, j: (0, 0)),
            pl.BlockSpec((1, D_MODEL), lambda i, j: (0, 0)),
            pl.BlockSpec((D_MODEL, IN_TN), lambda i, j: (0, j)),
        ],
        out_specs=[
            pl.BlockSpec((tm, IN_TN), lambda i, j: (i, jnp.minimum(j, IN_NR - 1))),
            pl.BlockSpec((tm, IN_TN), lambda i, j: (i, jnp.maximum(j - IN_NR, 0))),
        ],
        out_shape=[
            jax.ShapeDtypeStruct((n, RWKV_PCOLS), F32),
            jax.ShapeDtypeStruct((n, SB_PCOLS), BF16),
        ],
        scratch_shapes=[pltpu.VMEM((tm, D_MODEL), BF16)],
        compiler_params=_cparams(("arbitrary", "arbitrary")),
        name="ln_inproj",
    )(x, g, b, w)


SB_TQ = 256
SB_W = MXU_DIM
SB_HEADS = SB_W // HEAD_DIM
SB_GROUPS_PER_STEP = 2


def _sb_setup(carry_ref, acc_ref, tq):
    r = lax.broadcasted_iota(jnp.int32, (2 * SB_BLOCK, 2 * SB_BLOCK), 0) & (SB_BLOCK - 1)
    c = lax.broadcasted_iota(jnp.int32, (2 * SB_BLOCK, 2 * SB_BLOCK), 1)
    u = jnp.where((c >= SB_BLOCK) | (r >= c), -1.0, 0.0).astype(BF16)
    carry_ref[...] = jnp.zeros_like(carry_ref)
    acc_ref[...] = jnp.zeros_like(acc_ref)
    row = lax.broadcasted_iota(jnp.int32, (tq, SB_BLOCK), 0)
    col = lax.broadcasted_iota(jnp.int32, (tq, SB_BLOCK), 1)
    return u, row, col


def _sb_block_diag(x):
    rb = lax.broadcasted_iota(jnp.int32, (SB_HEADS * SB_BLOCK, SB_W), 0)
    cb = lax.broadcasted_iota(jnp.int32, (SB_HEADS * SB_BLOCK, SB_W), 1)
    bd_mask = (rb >> 7) == (cb >> LOG2_HEAD)
    return jnp.where(bd_mask, jnp.concatenate([x] * SB_HEADS, axis=0), jnp.zeros((), BF16))


def _sb_weights(z_all, mask, u, carry_ref, rows=slice(None)):
    z = [za[:, h * SB_BLOCK:(h + 1) * SB_BLOCK] for za in z_all for h in range(SB_HEADS)]
    m = _each(lambda a: jnp.maximum(a, 0.0) + jnp.log(1.0 + jnp.exp(_neg_abs(a))), z)
    if mask is not None:
        m = _each(lambda a: jnp.where(mask, a, 0.0), m)
    hl = _each(lambda a: jnp.concatenate(_split(a), axis=1), m)
    at = _each(lambda a: _dot(a, u), hl)
    carry = [carry_ref[h, rows, :] for h in range(len(z))]
    w = _each(lambda a, b, c: jnp.exp(a + b[:, :SB_BLOCK] + c), z, at, carry)
    if mask is not None:
        w = _each(lambda a: jnp.where(mask, a, 0.0), w)
    for h in range(len(z)):
        carry_ref[h, rows, :] = carry[h] + at[h][:, SB_BLOCK:]
    w = [a.astype(BF16) for a in w]
    return [jnp.concatenate(w[g * SB_HEADS:(g + 1) * SB_HEADS], axis=1) for g in range(len(z_all))]


def _sb_step(q, kb, vb, mask, u, carry_ref, acc_ref):
    w, = _sb_weights([_dot_nt(q, _sb_block_diag(kb))], mask, u, carry_ref)
    acc_ref[...] += _dot(w, _sb_block_diag(vb))


def _sb_finish(acc_ref, g_ref, o_ref):
    r = lax.broadcasted_iota(jnp.int32, (2 * SB_W, SB_W), 0) & (SB_W - 1)
    c = lax.broadcasted_iota(jnp.int32, (2 * SB_W, SB_W), 1)
    seg_ones = ((r >> LOG2_HEAD) == (c >> LOG2_HEAD)).astype(BF16)
    for g in range(acc_ref.shape[1] // SB_W):
        gs = slice(g * SB_W, (g + 1) * SB_W)
        acc = acc_ref[:, gs]
        ms = _dot(jnp.concatenate(_split(acc * acc), axis=1), seg_ones) * (1.0 / HEAD_DIM)
        o_ref[:, gs] = (acc * lax.rsqrt(ms + RMS_EPS) * g_ref[:, gs]).astype(o_ref.dtype)


def _sb_kernel(q_ref, k_ref, v_ref, km_ref, vm_ref, g_ref, o_ref, carry_ref, acc_ref,
               kbd_ref, vbd_ref, z0_ref, z1_ref, w0_ref, w1_ref, *, tq, n_blocks):
    qi = pl.program_id(2)
    assert tq == 2 * SB_BLOCK
    u, row, col = _sb_setup(carry_ref, acc_ref, tq)
    q = q_ref[...]
    groups = range(q.shape[1] // SB_W)
    gsl = [slice(g * SB_W, (g + 1) * SB_W) for g in groups]
    every = slice(None)

    @pl.when(qi == 0)
    def _():
        def fill(j, carry):
            s = pl.multiple_of(j * SB_BLOCK, SB_BLOCK)
            for g in groups:
                kbd_ref[g, j] = _sb_block_diag(k_ref[pl.ds(s, SB_BLOCK), gsl[g]])
                vbd_ref[g, j] = _sb_block_diag(v_ref[pl.ds(s, SB_BLOCK), gsl[g]])
            return carry
        lax.fori_loop(0, n_blocks, fill, 0)
        for g in groups:
            kbd_ref[g, n_blocks] = _sb_block_diag(km_ref[:, gsl[g]])
            vbd_ref[g, n_blocks] = _sb_block_diag(vm_ref[:, gsl[g]])

    def logits(blk, z_ref, rows=every):
        for g in groups:
            z_ref[g, rows, :] = _dot_nt(q[rows, gsl[g]], kbd_ref[g, blk])

    def weights(z_ref, w_ref, mask, rows=every):
        ws = _sb_weights([z_ref[g, rows, :] for g in groups], mask, u, carry_ref, rows)
        for g in groups:
            w_ref[g, rows, :] = ws[g]

    def pv(blk, w_ref, rows=every):
        for g in groups:
            acc_ref[rows, gsl[g]] += _dot(w_ref[g, rows, :], vbd_ref[g, blk])

    top = 2 * qi + 1
    low = slice(SB_BLOCK, tq)
    logits(top, z0_ref, low)
    logits(top - 1, z1_ref)
    weights(z0_ref, w0_ref, (col < row)[:SB_BLOCK], low)
    pv(top, w0_ref, low)
    logits(jnp.where(qi == 0, n_blocks, top - 2), z0_ref)
    weights(z1_ref, w1_ref, col < row)

    def body(m, carry):
        b0 = top - 2 - 2 * m
        logits(b0 - 1, z1_ref)
        pv(b0 + 1, w1_ref)
        weights(z0_ref, w0_ref, None)
        logits(jnp.where(m == qi - 1, n_blocks, b0 - 2), z0_ref)
        pv(b0, w0_ref)
        weights(z1_ref, w1_ref, None)
        return carry

    lax.fori_loop(0, qi, body, 0)
    pv(0, w1_ref)
    weights(z0_ref, w0_ref, col < N_META)
    pv(n_blocks, w0_ref)
    _sb_finish(acc_ref, g_ref, o_ref)


def _sb_meta_kernel(q_ref, k_ref, v_ref, g_ref, o_ref, carry_ref, acc_ref):
    u, row, col = _sb_setup(carry_ref, acc_ref, META_PAD)
    _sb_step(q_ref[...], k_ref[...], v_ref[...], col < row, u, carry_ref, acc_ref)
    _sb_finish(acc_ref, g_ref, o_ref)


def _sb_scratch(tq, n_groups=1):
    return [pltpu.VMEM((n_groups * SB_HEADS, tq, SB_BLOCK), F32), pltpu.VMEM((tq, n_groups * SB_W), F32)]


def _sb_attention(ps, ps_meta, norm_g, batch, seq):
    ng = SB_GROUPS_PER_STEP
    w = ng * SB_W
    n_g = SB_WIDTH // w
    nq = seq // SB_TQ
    n_blocks = seq // SB_BLOCK
    kern = functools.partial(_sb_kernel, tq=SB_TQ, n_blocks=n_blocks)
    bd_rows = SB_HEADS * SB_BLOCK
    pipeline_scratch = [
        pltpu.VMEM((ng, n_blocks + 1, bd_rows, SB_W), BF16),
        pltpu.VMEM((ng, n_blocks + 1, bd_rows, SB_W), BF16),
        pltpu.VMEM((ng, SB_TQ, bd_rows), F32), pltpu.VMEM((ng, SB_TQ, bd_rows), F32),
        pltpu.VMEM((ng, SB_TQ, bd_rows), BF16), pltpu.VMEM((ng, SB_TQ, bd_rows), BF16),
    ]
    return pl.pallas_call(
        kern,
        grid=(batch, n_g, nq),
        in_specs=[
            pl.BlockSpec((SB_TQ, w), lambda b, h, q: (b * nq + q, h)),
            pl.BlockSpec((seq, w), lambda b, h, q: (b, n_g + h)),
            pl.BlockSpec((seq, w), lambda b, h, q: (b, 2 * n_g + h)),
            pl.BlockSpec((META_PAD, w), lambda b, h, q: (0, n_g + h)),
            pl.BlockSpec((META_PAD, w), lambda b, h, q: (0, 2 * n_g + h)),
            pl.BlockSpec((1, w), lambda b, h, q: (0, h)),
        ],
        out_specs=pl.BlockSpec((SB_TQ, w), lambda b, h, q: (b * nq + q, h)),
        out_shape=jax.ShapeDtypeStruct((batch * seq, SB_WIDTH), BF16),
        scratch_shapes=_sb_scratch(SB_TQ, ng) + pipeline_scratch,
        compiler_params=_cparams(("arbitrary", "arbitrary", "arbitrary")),
        name="sb_attention",
    )(ps, ps, ps, ps_meta, ps_meta, norm_g)


def _sb_attention_meta(ps_meta, norm_g):
    w = SB_W
    n_g = SB_WIDTH // w
    return pl.pallas_call(
        _sb_meta_kernel,
        grid=(n_g,),
        in_specs=[
            pl.BlockSpec((META_PAD, w), lambda h: (0, h)),
            pl.BlockSpec((META_PAD, w), lambda h: (0, n_g + h)),
            pl.BlockSpec((META_PAD, w), lambda h: (0, 2 * n_g + h)),
            pl.BlockSpec((1, w), lambda h: (0, h)),
        ],
        out_specs=pl.BlockSpec((META_PAD, w), lambda h: (0, h)),
        out_shape=jax.ShapeDtypeStruct((META_PAD, SB_WIDTH), BF16),
        scratch_shapes=_sb_scratch(META_PAD),
        compiler_params=_cparams(("arbitrary",)),
        name="sb_attention_meta",
    )(ps_meta, ps_meta, ps_meta, norm_g)


def _bd(x, seg_ones):
    return jnp.concatenate([x.astype(BF16)] * HEADS_PER_GROUP, axis=0) * seg_ones


def _seg_sum(a, seg_ones):
    return _dot(a.astype(BF16), seg_ones)


def _mm_bd(a, b, seg_ones):
    return _dot(a.astype(BF16), _bd(b, seg_ones))


def _rwkv_kernel(pm_ref, pl_ref, prev_m_ref, prev_l_ref, s0_ref, mu_m_ref, mu_l_ref,
                 w0_ref, w2_ref, a0_ref, a2_ref, g2_ref, kk_ref, ka_ref, rk_ref, gng_ref, gnb_ref,
                 y_ref, s_out_ref, s_ref, carry_m_ref, carry_l_ref, *, n_valid):
    c = pl.program_id(1)
    L = CHUNK
    nb = pm_ref.shape[0]

    @pl.when(c == 0)
    def _():
        for bi in range(nb):
            s_ref[bi * N_GROUPS:(bi + 1) * N_GROUPS] = s0_ref[...]
            carry_m_ref[bi] = prev_m_ref[...]
            carry_l_ref[bi] = prev_l_ref[...]

    row = lax.broadcasted_iota(jnp.int32, (L, 1), 0)
    first = row == 0

    def token_shift(p, prev_last, mu):
        shifted = jnp.where(first, prev_last, pltpu.roll(p, 1, 0))
        return p + (shifted - p) * mu

    r4 = lax.broadcasted_iota(jnp.int32, (4 * L, GROUP), 0)
    c4 = lax.broadcasted_iota(jnp.int32, (4 * L, GROUP), 1)
    bd_mask = (r4 >> LOG2_CHUNK) == (c4 >> LOG2_HEAD)
    seg_ones = bd_mask.astype(BF16)
    ri = lax.broadcasted_iota(jnp.int32, (L, L), 0)
    ci = lax.broadcasted_iota(jnp.int32, (L, L), 1)
    tri_incl = (ci <= ri).astype(BF16)
    r2 = lax.broadcasted_iota(jnp.int32, (2 * L, 4 * L), 0)
    c2 = lax.broadcasted_iota(jnp.int32, (2 * L, 4 * L), 1) & (L - 1)
    low_mask = c2 < (r2 & (L - 1)) + (r2 >> LOG2_CHUNK)
    rl = lax.broadcasted_iota(jnp.int32, (L, 4 * L), 0)
    cl = lax.broadcasted_iota(jnp.int32, (L, 4 * L), 1) & (L - 1)
    eye_t = (rl == cl).astype(F32)
    consts = (row, first, bd_mask, seg_ones, tri_incl, low_mask, eye_t)

    gsl = [slice(g * GROUP, (g + 1) * GROUP) for g in range(N_GROUPS)]
    xm, lora = [], ([], [], [])
    for bi in range(nb):
        pl_blk = pl_ref[bi]
        xl = token_shift(pl_blk, carry_l_ref[bi, 7:8, :], mu_l_ref[...])
        carry_l_ref[bi] = pl_blk[L - 8:, :]
        parts = (jnp.tanh(xl[:, 0:LORA_W]).astype(BF16), xl[:, LORA_W:2 * LORA_W].astype(BF16),
                 _sigmoid(xl[:, 2 * LORA_W:]).astype(BF16))
        pm_blk = pm_ref[bi]
        xm_all = token_shift(pm_blk, carry_m_ref[bi, 7:8, :], mu_m_ref[...])
        carry_m_ref[bi] = pm_blk[L - 8:, :]
        xm += [xm_all[:, g * 3 * GROUP:(g + 1) * 3 * GROUP] for g in range(N_GROUPS)]
        for dst, part in zip(lora, parts):
            dst += [part] * N_GROUPS
    prm = [[ref[:, gs] for gs in gsl] for ref in
           (w0_ref, w2_ref, a0_ref, a2_ref, g2_ref, kk_ref, ka_ref, rk_ref, gng_ref, gnb_ref)]
    n_chains = nb * N_GROUPS
    ys, s_new = _rwkv_groups(xm, lora, [p * nb for p in prm], [s_ref[i] for i in range(n_chains)],
                             consts, n_valid)
    for i in range(n_chains):
        s_ref[i] = s_new[i]
        s_out_ref[i] = s_new[i]
        y_ref[i // N_GROUPS, :, gsl[i % N_GROUPS]] = ys[i].astype(y_ref.dtype)


def _rwkv_groups(xm, lora, prm, s_t, consts, n_valid):
    L = CHUNK
    row, first, bd_mask, seg_ones, tri_incl, low_mask, eye_t = consts
    w0, w2, a0, a2, g2, k_k, k_a, r_k, gn_g, gn_b = prm
    tanh_wl, al, sig_gl = lora
    r = [x[:, 0:GROUP] for x in xm]
    k = [x[:, GROUP:2 * GROUP] for x in xm]
    v = [x[:, 2 * GROUP:3 * GROUP] for x in xm]

    dw = _each(lambda b, x, w: b + _dot(x, w), w0, tanh_wl, w2)
    w_log = _each(lambda d: jnp.minimum(d, 0.0) - jnp.log(1.0 + jnp.exp(_neg_abs(d))) - 0.5, dw)
    logw = _each(lambda x: -jnp.exp(x), w_log)
    iclr = _each(lambda b, x, w: _sigmoid(b + _dot(x, w)), a0, al, a2)
    gate = _each(_dot, sig_gl, g2)

    kk = _each(lambda a, b: a * b, k, k_k)
    kk = _each(lambda x: x * lax.rsqrt(jnp.maximum(_seg_sum(x * x, seg_ones), 1e-24)), kk)
    kmod = _each(lambda x, i, a: x * (1.0 + (i - 1.0) * a), k, iclr, k_a)

    if n_valid < L:
        valid = row < n_valid
        zero_pad = lambda x: jnp.where(valid, x, 0.0)
        kk, kmod, v, logw = _each(zero_pad, kk), _each(zero_pad, kmod), _each(zero_pad, v), _each(zero_pad, logw)

    cum = _each(lambda x: _mm_hi_lhs_const(tri_incl, x), logw)
    p_in = _each(jnp.exp, cum)
    p_ex = _each(lambda c, w: jnp.exp(c - w), cum, logw)
    p_inv = _each(lambda c: jnp.exp(-c), cum)
    p_tail = _each(lambda c: jnp.exp(c[L - 1:L, :] - c), cum)
    p_last = _each(lambda c: jnp.exp(c[L - 1:L, :]), cum)
    at = _each(lambda a, p: -a * p, kk, p_ex)
    kb = _each(lambda a, i: a * i, kk, iclr)
    bt = _each(lambda a, p: (a * p).astype(BF16), kb, p_inv)
    kt = _each(lambda a, p: (a * p).astype(BF16), kmod, p_inv)
    rt = _each(lambda a, p: a * p, r, p_in)

    ar = _each(lambda a, b: jnp.concatenate([a, b], axis=0).astype(BF16), at, rt)
    abr = _each(lambda a, b: jnp.where(low_mask, _dot_nt(a, _bd(b, seg_ones)), 0.0), ar, bt)
    akr = _each(lambda a, b: jnp.where(low_mask, _dot_nt(a, _bd(b, seg_ones)), 0.0), ar, kt)
    sa = _each(lambda a, s: _dot_nt(a, s.astype(BF16)), ar, s_t)
    bd_v = _each(lambda x: _bd(x, seg_ones), v)
    x = _each(lambda s, a, b: s[:L] + _dot(a[:L].astype(BF16), b), sa, akr, bd_v)

    acc = [eye_t] * len(xm)
    pw = [a[:L] for a in abr]
    for lvl in range(LOG2_CHUNK):
        if lvl < LOG2_CHUNK - 1:
            res = _each(lambda a, p: _mm_bd(jnp.concatenate([a, p], axis=0), p, seg_ones), acc, pw)
            acc = _each(lambda a, rr: a + rr[:L], acc, res)
            pw = [rr[L:] for rr in res]
        else:
            acc = _each(lambda a, p: a + _mm_bd(a, p, seg_ones), acc, pw)
    u = _each(lambda a, b: _mm_bd(a, b, seg_ones), acc, x)

    rbk = _each(lambda a, b: jnp.concatenate([a[L:], b[L:]], axis=1).astype(BF16), abr, akr)
    uv_bd = _each(lambda a, b: jnp.concatenate([_bd(a, seg_ones), b], axis=0), u, bd_v)
    y = _each(lambda s, a, b: s[L:] + _dot(a, b), sa, rbk, uv_bd)

    uv_t = _each(lambda a, b: jnp.transpose(jnp.concatenate([a, b], axis=0)).astype(BF16), u, v)
    bk_tail = _each(lambda a, b, p: jnp.concatenate([a * p, b * p], axis=0).astype(BF16), kb, kmod, p_tail)
    s_new = _each(lambda s, p, a, b: s * p + jnp.where(bd_mask, _dot(a, b), 0.0), s_t, p_last, uv_t, bk_tail)

    mean = _each(lambda a: _seg_sum(a, seg_ones) * (1.0 / HEAD_DIM), y)
    yc = _each(lambda a, m: a - m, y, mean)
    var = _each(lambda a: _seg_sum(a * a, seg_ones) * (1.0 / HEAD_DIM), yc)
    yn = _each(lambda a, s, g_, b_: a * lax.rsqrt(s + GN_EPS) * g_ + b_, yc, var, gn_g, gn_b)
    bonus = _each(lambda a, b, c, d: _seg_sum(a * b * c, seg_ones) * d, r, kmod, r_k, v)
    out = _each(lambda a, b, c: (a + b) * c, yn, bonus, gate)
    return out, s_new


def _mm_hi_lhs_const(a_bf16, b):
    hi, lo = _split(b)
    return _dot(a_bf16, hi) + _dot(a_bf16, lo)


def _rwkv(p, prev, prev_blk, s0, prm, nb, n_valid):
    batch, rows, _ = p.shape
    n_chunks = rows // CHUNK
    n_main = 3 * RWKV_WIDTH
    lora_blk = n_main // LORA_COLS
    row_map = lambda b, c: (b, c, 0)
    whole = lambda b, c: (0, 0)
    vec = pl.BlockSpec((1, RWKV_WIDTH), whole)
    kern = functools.partial(_rwkv_kernel, n_valid=n_valid)
    return pl.pallas_call(
        kern,
        grid=(batch // nb, n_chunks),
        in_specs=[
            pl.BlockSpec((nb, CHUNK, n_main), row_map),
            pl.BlockSpec((nb, CHUNK, LORA_COLS), lambda b, c: (b, c, lora_blk)),
            pl.BlockSpec((8, n_main), lambda b, c: (prev_blk, 0)),
            pl.BlockSpec((8, LORA_COLS), lambda b, c: (prev_blk, lora_blk)),
            pl.BlockSpec((N_GROUPS, GROUP, GROUP), lambda b, c: (0, 0, 0)),
            pl.BlockSpec((1, n_main), whole),
            pl.BlockSpec((1, LORA_COLS), whole),
            vec,
            pl.BlockSpec((LORA_W, RWKV_WIDTH), whole),
            vec,
            pl.BlockSpec((LORA_W, RWKV_WIDTH), whole),
            pl.BlockSpec((LORA_G, RWKV_WIDTH), whole),
            vec, vec, vec, vec, vec,
        ],
        out_specs=[
            pl.BlockSpec((nb, CHUNK, RWKV_WIDTH), row_map),
            pl.BlockSpec((nb * N_GROUPS, GROUP, GROUP), lambda b, c: (b, 0, 0)),
        ],
        out_shape=[
            jax.ShapeDtypeStruct((batch, rows, RWKV_WIDTH), BF16),
            jax.ShapeDtypeStruct((batch * N_GROUPS, GROUP, GROUP), F32),
        ],
        scratch_shapes=[
            pltpu.VMEM((nb * N_GROUPS, GROUP, GROUP), F32),
            pltpu.VMEM((nb, 8, n_main), F32),
            pltpu.VMEM((nb, 8, LORA_COLS), F32),
        ],
        compiler_params=_cparams(("arbitrary", "arbitrary")),
        name="rwkv7",
    )(p, p, prev, prev, s0, prm["mu_m"], prm["mu_l"], prm["w0"], prm["w2"], prm["a0"], prm["a2"],
      prm["g2"], prm["k_k"], prm["k_a"], prm["r_k"], prm["gn_g"], prm["gn_b"])


def _outproj_ln_kernel(yr_ref, ys_ref, x_ref, w_ref, g0_ref, b0_ref, g_ref, b_ref, o_ref, ob_ref):
    h = _layer_norm(x_ref[...], g0_ref[...], b0_ref[...])
    mix = _dot(yr_ref[...], w_ref[0:RWKV_WIDTH, :]) + _dot(ys_ref[...], w_ref[RWKV_WIDTH:, :])
    h1 = _layer_norm(DEEPNORM_ALPHA * h + mix, g_ref[...], b_ref[...])
    o_ref[...] = h1
    ob_ref[...] = h1.astype(BF16)


def _outproj_ln(yr, ys, x, w, g0, b0, g, b, tm):
    n = x.shape[0]
    return pl.pallas_call(
        _outproj_ln_kernel,
        grid=(n // tm,),
        in_specs=[
            pl.BlockSpec((tm, RWKV_WIDTH), lambda i: (i, 0)),
            pl.BlockSpec((tm, SB_WIDTH), lambda i: (i, 0)),
            pl.BlockSpec((tm, D_MODEL), lambda i: (i, 0)),
            pl.BlockSpec((D_MODEL, D_MODEL), lambda i: (0, 0)),
            pl.BlockSpec((1, D_MODEL), lambda i: (0, 0)),
            pl.BlockSpec((1, D_MODEL), lambda i: (0, 0)),
            pl.BlockSpec((1, D_MODEL), lambda i: (0, 0)),
            pl.BlockSpec((1, D_MODEL), lambda i: (0, 0)),
        ],
        out_specs=[pl.BlockSpec((tm, D_MODEL), lambda i: (i, 0)), pl.BlockSpec((tm, D_MODEL), lambda i: (i, 0))],
        out_shape=[jax.ShapeDtypeStruct((n, D_MODEL), F32), jax.ShapeDtypeStruct((n, D_MODEL), BF16)],
        compiler_params=_cparams(("arbitrary",)),
        name="outproj_ln",
    )(yr, ys, x, w, g0, b0, g, b)


FF_TF = 512
FF_NF = D_FF // FF_TF


def _gate_halo_kernel(h_ref, w_ref, o_ref):
    o_ref[0] = _dot(h_ref[...].astype(BF16), w_ref[...])


def _gate_halo(h_rows, w_up):
    return pl.pallas_call(
        _gate_halo_kernel,
        grid=(FF_NF,),
        in_specs=[
            pl.BlockSpec((8, D_MODEL), lambda f: (N_META // 8 - 1, 0)),
            pl.BlockSpec((D_MODEL, FF_TF), lambda f: (0, f)),
        ],
        out_specs=pl.BlockSpec((1, 8, FF_TF), lambda f: (f, 0, 0)),
        out_shape=jax.ShapeDtypeStruct((FF_NF, 8, FF_TF), F32),
        compiler_params=_cparams(("arbitrary",)),
        name="ffn_gate_halo",
    )(h_rows, w_up)


def _ffn_kernel(h_ref, hb_ref, halo0_ref, wg_ref, wv_ref, cw_ref, cb_ref, wd_ref, g_ref, b_ref, o_ref,
                acc_ref, halo_ref, *, tiles_per_seq):
    i = pl.program_id(0)
    f = pl.program_id(1)
    tm = h_ref.shape[0]

    @pl.when((i == 0) & (f == 0))
    def _():
        acc_ref[...] = jnp.zeros_like(acc_ref)

    @pl.when((i % tiles_per_seq) == 0)
    def _():
        halo_ref[f] = halo0_ref[0]

    hb = hb_ref[...]
    row = lax.broadcasted_iota(jnp.int32, (tm, 1), 0)
    halves = [slice(s * MXU_DIM, (s + 1) * MXU_DIM) for s in range(FF_TF // MXU_DIM)]
    gate = [_dot(hb, wg_ref[:, cs]) for cs in halves]
    val = [_dot(hb, wv_ref[:, cs]) for cs in halves]
    down = []
    for cs, g, v in zip(halves, gate, val):
        halo = halo_ref[f, :, cs]
        halo_ref[f, :, cs] = g[tm - 8:, :]
        g1 = jnp.where(row == 0, halo[7:8, :], pltpu.roll(g, 1, 0))
        g2 = jnp.where(row == 0, halo[6:7, :], jnp.where(row == 1, halo[7:8, :], pltpu.roll(g, 2, 0)))
        conv = cw_ref[0:1, cs] * g2 + cw_ref[1:2, cs] * g1 + cw_ref[2:3, cs] * g + cb_ref[:, cs]
        act = (conv * _sigmoid(conv) * v).astype(BF16)
        down.append(_dot(act, wd_ref[cs, :]))
    acc_ref[...] = jnp.where(f == 0, 0.0, acc_ref[...]) + functools.reduce(lambda a, b: a + b, down)

    @pl.when(f == FF_NF - 1)
    def _():
        o_ref[...] = _layer_norm(DEEPNORM_ALPHA * h_ref[...] + acc_ref[...], g_ref[...], b_ref[...])


def _ffn(h, h_bf16, halo0, w_up, conv_w, conv_b, w_down, g, b, tm, seq):
    n = h.shape[0]
    kern = functools.partial(_ffn_kernel, tiles_per_seq=seq // tm)
    return pl.pallas_call(
        kern,
        grid=(n // tm, FF_NF),
        in_specs=[
            pl.BlockSpec((tm, D_MODEL), lambda i, f: (i, 0)),
            pl.BlockSpec((tm, D_MODEL), lambda i, f: (i, 0)),
            pl.BlockSpec((1, 8, FF_TF), lambda i, f: (f, 0, 0)),
            pl.BlockSpec((D_MODEL, FF_TF), lambda i, f: (0, f)),
            pl.BlockSpec((D_MODEL, FF_TF), lambda i, f: (0, FF_NF + f)),
            pl.BlockSpec((3, FF_TF), lambda i, f: (0, f)),
            pl.BlockSpec((1, FF_TF), lambda i, f: (0, f)),
            pl.BlockSpec((FF_TF, D_MODEL), lambda i, f: (f, 0)),
            pl.BlockSpec((1, D_MODEL), lambda i, f: (0, 0)),
            pl.BlockSpec((1, D_MODEL), lambda i, f: (0, 0)),
        ],
        out_specs=pl.BlockSpec((tm, D_MODEL), lambda i, f: (i, 0)),
        out_shape=jax.ShapeDtypeStruct((n, D_MODEL), F32),
        scratch_shapes=[
            pltpu.VMEM((tm, D_MODEL), F32),
            pltpu.VMEM((FF_NF, 8, FF_TF), F32),
        ],
        compiler_params=_cparams(("arbitrary", "arbitrary")),
        name="conv_ffn",
    )(h, h_bf16, halo0, w_up, w_up, conv_w, conv_b, w_down, g, b)


def _group_major(a):
    lead = a.shape[:-1]
    a = a.reshape(*lead, 3, N_GROUPS, GROUP)
    return jnp.swapaxes(a, -3, -2).reshape(*lead, 3 * RWKV_WIDTH)


def _pad_cols(a, width):
    return jnp.pad(a, [(0, 0)] * (a.ndim - 1) + [(0, width - a.shape[-1])])


def _pad_rows(a, rows):
    return jnp.pad(a, [(0, rows - a.shape[0])] + [(0, 0)] * (a.ndim - 1))


def _lora_slots(a):
    c = 3 * RWKV_WIDTH
    wl = a[..., c:c + DECAY_LORA]
    al = a[..., c + DECAY_LORA:c + DECAY_LORA + ICLR_LORA]
    gl = a[..., c + DECAY_LORA + ICLR_LORA:c + DECAY_LORA + ICLR_LORA + GATE_LORA]
    return jnp.concatenate([_pad_cols(wl, LORA_W), _pad_cols(al, LORA_W), _pad_cols(gl, LORA_G)], axis=-1)


def kernel(x, meta_tokens, emb_ln_g, emb_ln_b, w_in, rwkv_mu, rwkv_w0, rwkv_w2, rwkv_a0, rwkv_a2, rwkv_g2, rwkv_k_k, rwkv_k_a, rwkv_r_k, rwkv_gn_g, rwkv_gn_b, sb_norm_g, w_out, ln1_g, ln1_b, ffn_w_up, ffn_conv_w, ffn_conv_b, ffn_w_down, ln2_g, ln2_b):
    batch, seq, d = x.shape
    assert d == D_MODEL and seq % 512 == 0 and w_in.shape[0] == 1 and batch % RWKV_BATCH_PER_STEP == 0
    n_rw = 3 * RWKV_WIDTH + DECAY_LORA + ICLR_LORA + GATE_LORA
    row2 = lambda a: a.reshape(1, -1)
    grp3 = lambda a: a.reshape(N_GROUPS, 1, GROUP)

    wi = w_in[0].astype(BF16)
    sb_cols = wi[:, n_rw:]
    sb_cols = jnp.concatenate([sb_cols[:, :SB_WIDTH] * (HEAD_DIM ** -0.5), sb_cols[:, SB_WIDTH:]], axis=1)
    w1 = jnp.concatenate([_group_major(wi[:, :3 * RWKV_WIDTH]), _lora_slots(wi), sb_cols], axis=1)
    mu = rwkv_mu[0]
    prm = {
        "mu_m": row2(_group_major(mu[:3 * RWKV_WIDTH])),
        "mu_l": row2(_lora_slots(mu)),
        "w0": row2(rwkv_w0[0]),
        "w2": _pad_rows(rwkv_w2[0], LORA_W).astype(BF16),
        "a0": row2(rwkv_a0[0]),
        "a2": _pad_rows(rwkv_a2[0], LORA_W).astype(BF16),
        "g2": _pad_rows(rwkv_g2[0], LORA_G).astype(BF16),
        "k_k": row2(rwkv_k_k[0]),
        "k_a": row2(rwkv_k_a[0]),
        "r_k": row2(rwkv_r_k[0]),
        "gn_g": row2(rwkv_gn_g[0]),
        "gn_b": row2(rwkv_gn_b[0]),
    }
    w_o = w_out[0].astype(BF16)
    w_up = ffn_w_up[0].astype(BF16)
    w_dn = ffn_w_down[0].astype(BF16)
    g0, b0 = row2(emb_ln_g), row2(emb_ln_b)
    g1, b1 = row2(ln1_g[0]), row2(ln1_b[0])
    g2, b2 = row2(ln2_g[0]), row2(ln2_b[0])
    sbg = row2(sb_norm_g[0])

    xm = _pad_rows(meta_tokens.astype(x.dtype), META_PAD)
    pr_m, ps_m = _ln_inproj(xm, g0, b0, w1, META_PAD)
    ysb_m = _sb_attention_meta(ps_m, sbg)
    zeros_state = jnp.zeros((N_GROUPS, GROUP, GROUP), F32)
    yr_m, s_meta = _rwkv(pr_m[:CHUNK].reshape(1, CHUNK, RWKV_PCOLS), jnp.zeros((8, RWKV_PCOLS), F32), 0,
                         zeros_state, prm, 1, N_META)
    h1_m, _ = _outproj_ln(_pad_rows(yr_m[0], META_PAD), ysb_m, xm, w_o, g0, b0, g1, b1, META_PAD)
    halo0 = _gate_halo(h1_m, w_up)

    xr = x.reshape(batch * seq, d)
    pr, ps = _ln_inproj(xr, g0, b0, w1, 1024)
    y_sb = _sb_attention(ps, ps_m, sbg, batch, seq)
    y_rw, _ = _rwkv(pr.reshape(batch, seq, RWKV_PCOLS), pr_m, N_META // 8 - 1, s_meta, prm,
                    RWKV_BATCH_PER_STEP, CHUNK)
    h1, h1_bf16 = _outproj_ln(y_rw.reshape(batch * seq, RWKV_WIDTH), y_sb, xr, w_o, g0, b0, g1, b1, 512)
    out = _ffn(h1, h1_bf16, halo0, w_up, ffn_conv_w[0], row2(ffn_conv_b[0]), w_dn, g2, b2, 512, seq)
    return out.reshape(batch, seq, d)
```

```python
import functools

import jax
import jax.numpy as jnp
from jax import lax
from jax.experimental import pallas as pl
from jax.experimental.pallas import tpu as pltpu

F32 = jnp.float32
BF16 = jnp.bfloat16

D_MODEL = 2048
N_META = 16
HEAD_DIM = 64
RWKV_WIDTH = 1024
SB_WIDTH = 1024
DECAY_LORA = 64
ICLR_LORA = 64
GATE_LORA = 160
D_FF = 5632
DEPTH = 1
DEEPNORM_ALPHA = (2.0 * DEPTH) ** 0.25
LN_EPS = 1e-5
GN_EPS = 64e-5
RMS_EPS = 1e-6

LANES = 128
MXU_DIM = 256

GROUP = MXU_DIM
N_GROUPS = RWKV_WIDTH // GROUP
HEADS_PER_GROUP = GROUP // HEAD_DIM
CHUNK = 64
LOG2_CHUNK = 6
LOG2_HEAD = 6
LOG2_META = 4
RWKV_BATCH_PER_STEP = 4
LORA_W = 128
LORA_G = 256
LORA_COLS = 2 * LORA_W + LORA_G
RWKV_PCOLS = 3 * RWKV_WIDTH + LORA_COLS
SB_PCOLS = 3 * SB_WIDTH
SB_BLOCK = 128
META_PAD = 128

VMEM_LIMIT = 56 * 1024 * 1024


def _cparams(sem, flags=None):
    return pltpu.CompilerParams(dimension_semantics=sem, vmem_limit_bytes=VMEM_LIMIT, flags=flags)


def _layer_norm(x, g, b):
    mu = jnp.mean(x, axis=-1, keepdims=True)
    xc = x - mu
    var = jnp.mean(xc * xc, axis=-1, keepdims=True)
    return xc * lax.rsqrt(var + LN_EPS) * g + b


def _dot(a, b):
    return jnp.dot(a, b, preferred_element_type=F32)


def _dot_nt(a, b):
    return lax.dot_general(a, b, (((1,), (1,)), ((), ())), preferred_element_type=F32)


def _split(x):
    hi = x.astype(BF16)
    lo = (x - hi.astype(F32)).astype(BF16)
    return hi, lo


def _neg_abs(x):
    bits = lax.bitcast_convert_type(x, jnp.uint32) | jnp.uint32(0x80000000)
    return lax.bitcast_convert_type(bits, F32)


def _sigmoid(x):
    return 1.0 / (1.0 + jnp.exp(-x))


def _each(f, *lists):
    return [f(*args) for args in zip(*lists)]


IN_TN = 512
IN_NR = RWKV_PCOLS // IN_TN
IN_NS = SB_PCOLS // IN_TN


def _ln_inproj_kernel(x_ref, g_ref, b_ref, w_ref, pr_ref, ps_ref, hb_ref):
    j = pl.program_id(1)

    @pl.when(j == 0)
    def _():
        hb_ref[...] = _layer_norm(x_ref[...], g_ref[...], b_ref[...]).astype(BF16)

    acc = _dot(hb_ref[...], w_ref[...])

    @pl.when(j < IN_NR)
    def _():
        pr_ref[...] = acc

    @pl.when(j >= IN_NR)
    def _():
        ps_ref[...] = acc.astype(BF16)


def _ln_inproj(x, g, b, w, tm):
    n = x.shape[0]
    return pl.pallas_call(
        _ln_inproj_kernel,
        grid=(n // tm, IN_NR + IN_NS),
        in_specs=[
            pl.BlockSpec((tm, D_MODEL), lambda i, j: (i, 0)),
            pl.BlockSpec((1, D_MODEL), lambda i, j: (0, 0)),
            pl.BlockSpec((1, D_MODEL), lambda i, j: (0, 0)),
            pl.BlockSpec((D_MODEL, IN_TN), lambda i, j: (0, j)),
        ],
        out_specs=[
            pl.BlockSpec((tm, IN_TN), lambda i, j: (i, jnp.minimum(j, IN_NR - 1))),
            pl.BlockSpec((tm, IN_TN), lambda i, j: (i, jnp.maximum(j - IN_NR, 0))),
        ],
        out_shape=[
            jax.ShapeDtypeStruct((n, RWKV_PCOLS), F32),
            jax.ShapeDtypeStruct((n, SB_PCOLS), BF16),
        ],
        scratch_shapes=[pltpu.VMEM((tm, D_MODEL), BF16)],
        compiler_params=_cparams(("arbitrary", "arbitrary")),
        name="ln_inproj",
    )(x, g, b, w)


SB_TQ = 256
SB_W = MXU_DIM
SB_HEADS = SB_W // HEAD_DIM
SB_GROUPS_PER_STEP = 2


def _sb_setup(carry_ref, acc_ref, tq):
    r = lax.broadcasted_iota(jnp.int32, (2 * SB_BLOCK, 2 * SB_BLOCK), 0) & (SB_BLOCK - 1)
    c = lax.broadcasted_iota(jnp.int32, (2 * SB_BLOCK, 2 * SB_BLOCK), 1)
    u = jnp.where((c >= SB_BLOCK) | (r >= c), -1.0, 0.0).astype(BF16)
    carry_ref[...] = jnp.zeros_like(carry_ref)
    acc_ref[...] = jnp.zeros_like(acc_ref)
    row = lax.broadcasted_iota(jnp.int32, (tq, SB_BLOCK), 0)
    col = lax.broadcasted_iota(jnp.int32, (tq, SB_BLOCK), 1)
    return u, row, col


def _sb_block_diag(x):
    rb = lax.broadcasted_iota(jnp.int32, (SB_HEADS * SB_BLOCK, SB_W), 0)
    cb = lax.broadcasted_iota(jnp.int32, (SB_HEADS * SB_BLOCK, SB_W), 1)
    bd_mask = (rb >> 7) == (cb >> LOG2_HEAD)
    return jnp.where(bd_mask, jnp.concatenate([x] * SB_HEADS, axis=0), jnp.zeros((), BF16))


def _sb_weights(z_all, mask, u, carry_ref, rows=slice(None)):
    z = [za[:, h * SB_BLOCK:(h + 1) * SB_BLOCK] for za in z_all for h in range(SB_HEADS)]
    m = _each(lambda a: jnp.maximum(a, 0.0) + jnp.log(1.0 + jnp.exp(_neg_abs(a))), z)
    if mask is not None:
        m = _each(lambda a: jnp.where(mask, a, 0.0), m)
    hl = _each(lambda a: jnp.concatenate(_split(a), axis=1), m)
    at = _each(lambda a: _dot(a, u), hl)
    carry = [carry_ref[h, rows, :] for h in range(len(z))]
    w = _each(lambda a, b, c: jnp.exp(a + b[:, :SB_BLOCK] + c), z, at, carry)
    if mask is not None:
        w = _each(lambda a: jnp.where(mask, a, 0.0), w)
    for h in range(len(z)):
        carry_ref[h, rows, :] = carry[h] + at[h][:, SB_BLOCK:]
    w = [a.astype(BF16) for a in w]
    return [jnp.concatenate(w[g * SB_HEADS:(g + 1) * SB_HEADS], axis=1) for g in range(len(z_all))]


def _sb_step(q, kb, vb, mask, u, carry_ref, acc_ref):
    w, = _sb_weights([_dot_nt(q, _sb_block_diag(kb))], mask, u, carry_ref)
    acc_ref[...] += _dot(w, _sb_block_diag(vb))


def _sb_finish(acc_ref, g_ref, o_ref):
    r = lax.broadcasted_iota(jnp.int32, (2 * SB_W, SB_W), 0) & (SB_W - 1)
    c = lax.broadcasted_iota(jnp.int32, (2 * SB_W, SB_W), 1)
    seg_ones = ((r >> LOG2_HEAD) == (c >> LOG2_HEAD)).astype(BF16)
    for g in range(acc_ref.shape[1] // SB_W):
        gs = slice(g * SB_W, (g + 1) * SB_W)
        acc = acc_ref[:, gs]
        ms = _dot(jnp.concatenate(_split(acc * acc), axis=1), seg_ones) * (1.0 / HEAD_DIM)
        o_ref[:, gs] = (acc * lax.rsqrt(ms + RMS_EPS) * g_ref[:, gs]).astype(o_ref.dtype)


def _sb_kernel(q_ref, k_ref, v_ref, km_ref, vm_ref, g_ref, o_ref, carry_ref, acc_ref,
               kbd_ref, vbd_ref, kmeta_ref, vmeta_ref, z0_ref, z1_ref, w0_ref, w1_ref, *, tq, n_blocks):
    qi = pl.program_id(2)
    assert tq == 2 * SB_BLOCK
    u, row, col = _sb_setup(carry_ref, acc_ref, tq)
    q = q_ref[...]
    groups = range(q.shape[1] // SB_W)
    gsl = [slice(g * SB_W, (g + 1) * SB_W) for g in groups]
    n_heads = len(groups) * SB_HEADS
    assert n_heads * N_META == SB_BLOCK
    every = slice(None)

    @pl.when(qi == 0)
    def _():
        def fill(j, carry):
            s = pl.multiple_of(j * SB_BLOCK, SB_BLOCK)
            for g in groups:
                kbd_ref[g, j] = _sb_block_diag(k_ref[pl.ds(s, SB_BLOCK), gsl[g]])
                vbd_ref[g, j] = _sb_block_diag(v_ref[pl.ds(s, SB_BLOCK), gsl[g]])
            return carry
        lax.fori_loop(0, n_blocks, fill, 0)
        rm = lax.broadcasted_iota(jnp.int32, (n_heads * N_META, q.shape[1]), 0)
        cm = lax.broadcasted_iota(jnp.int32, (n_heads * N_META, q.shape[1]), 1)
        own = (rm >> LOG2_META) == (cm >> LOG2_HEAD)
        for src_ref, dst_ref in ((km_ref, kmeta_ref), (vm_ref, vmeta_ref)):
            tiled = jnp.concatenate([src_ref[0:N_META, :]] * n_heads, axis=0)
            dst_ref[...] = jnp.where(own, tiled, jnp.zeros((), BF16))

    def logits(blk, z_ref, rows=every):
        for g in groups:
            z_ref[g, rows, :] = _dot_nt(q[rows, gsl[g]], kbd_ref[g, blk])

    def weights(z_ref, w_ref, mask, rows=every):
        ws = _sb_weights([z_ref[g, rows, :] for g in groups], mask, u, carry_ref, rows)
        for g in groups:
            w_ref[g, rows, :] = ws[g]

    def pv(blk, w_ref, rows=every):
        for g in groups:
            acc_ref[rows, gsl[g]] += _dot(w_ref[g, rows, :], vbd_ref[g, blk])

    top = 2 * qi + 1
    low = slice(SB_BLOCK, tq)
    logits(top, z0_ref, low)
    logits(top - 1, z1_ref)
    weights(z0_ref, w0_ref, (col < row)[:SB_BLOCK], low)
    pv(top, w0_ref, low)
    logits(jnp.maximum(top - 2, 0), z0_ref)
    weights(z1_ref, w1_ref, col < row)

    def body(m, carry):
        b0 = top - 2 - 2 * m
        logits(b0 - 1, z1_ref)
        pv(b0 + 1, w1_ref)
        weights(z0_ref, w0_ref, None)
        logits(jnp.maximum(b0 - 2, 0), z0_ref)
        pv(b0, w0_ref)
        weights(z1_ref, w1_ref, None)
        return carry

    lax.fori_loop(0, qi, body, 0)
    pv(0, w1_ref)

    zm = _dot_nt(q, kmeta_ref[...])
    mm = jnp.maximum(zm, 0.0) + jnp.log(1.0 + jnp.exp(_neg_abs(zm)))
    r2 = lax.broadcasted_iota(jnp.int32, (2 * SB_BLOCK, SB_BLOCK), 0) & (SB_BLOCK - 1)
    c2 = lax.broadcasted_iota(jnp.int32, (2 * SB_BLOCK, SB_BLOCK), 1)
    same_head = (r2 >> LOG2_META) == (c2 >> LOG2_META)
    um = jnp.where(same_head & ((r2 & (N_META - 1)) >= (c2 & (N_META - 1))), -1.0, 0.0).astype(BF16)
    after = _dot(jnp.concatenate(_split(mm), axis=1), um)
    head_of_lane = lax.broadcasted_iota(jnp.int32, (tq, SB_BLOCK), 1) >> LOG2_META
    carry = carry_ref[0]
    for h in range(1, n_heads):
        carry = jnp.where(head_of_lane == h, carry_ref[h], carry)
    wm = jnp.exp(zm + after + carry).astype(BF16)
    acc_ref[...] += _dot(wm, vmeta_ref[...])
    _sb_finish(acc_ref, g_ref, o_ref)


def _sb_meta_kernel(q_ref, k_ref, v_ref, g_ref, o_ref, carry_ref, acc_ref):
    u, row, col = _sb_setup(carry_ref, acc_ref, META_PAD)
    _sb_step(q_ref[...], k_ref[...], v_ref[...], col < row, u, carry_ref, acc_ref)
    _sb_finish(acc_ref, g_ref, o_ref)


def _sb_scratch(tq, n_groups=1):
    return [pltpu.VMEM((n_groups * SB_HEADS, tq, SB_BLOCK), F32), pltpu.VMEM((tq, n_groups * SB_W), F32)]


def _sb_attention(ps, ps_meta, norm_g, batch, seq):
    ng = SB_GROUPS_PER_STEP
    w = ng * SB_W
    n_g = SB_WIDTH // w
    nq = seq // SB_TQ
    n_blocks = seq // SB_BLOCK
    kern = functools.partial(_sb_kernel, tq=SB_TQ, n_blocks=n_blocks)
    bd_rows = SB_HEADS * SB_BLOCK
    pipeline_scratch = [
        pltpu.VMEM((ng, n_blocks, bd_rows, SB_W), BF16),
        pltpu.VMEM((ng, n_blocks, bd_rows, SB_W), BF16),
        pltpu.VMEM((SB_BLOCK, w), BF16), pltpu.VMEM((SB_BLOCK, w), BF16),
        pltpu.VMEM((ng, SB_TQ, bd_rows), F32), pltpu.VMEM((ng, SB_TQ, bd_rows), F32),
        pltpu.VMEM((ng, SB_TQ, bd_rows), BF16), pltpu.VMEM((ng, SB_TQ, bd_rows), BF16),
    ]
    return pl.pallas_call(
        kern,
        grid=(batch, n_g, nq),
        in_specs=[
            pl.BlockSpec((SB_TQ, w), lambda b, h, q: (b * nq + q, h)),
            pl.BlockSpec((seq, w), lambda b, h, q: (b, n_g + h)),
            pl.BlockSpec((seq, w), lambda b, h, q: (b, 2 * n_g + h)),
            pl.BlockSpec((META_PAD, w), lambda b, h, q: (0, n_g + h)),
            pl.BlockSpec((META_PAD, w), lambda b, h, q: (0, 2 * n_g + h)),
            pl.BlockSpec((1, w), lambda b, h, q: (0, h)),
        ],
        out_specs=pl.BlockSpec((SB_TQ, w), lambda b, h, q: (b * nq + q, h)),
        out_shape=jax.ShapeDtypeStruct((batch * seq, SB_WIDTH), BF16),
        scratch_shapes=_sb_scratch(SB_TQ, ng) + pipeline_scratch,
        compiler_params=_cparams(("arbitrary", "arbitrary", "arbitrary")),
        name="sb_attention",
    )(ps, ps, ps, ps_meta, ps_meta, norm_g)


def _sb_attention_meta(ps_meta, norm_g):
    w = SB_W
    n_g = SB_WIDTH // w
    return pl.pallas_call(
        _sb_meta_kernel,
        grid=(n_g,),
        in_specs=[
            pl.BlockSpec((META_PAD, w), lambda h: (0, h)),
            pl.BlockSpec((META_PAD, w), lambda h: (0, n_g + h)),
            pl.BlockSpec((META_PAD, w), lambda h: (0, 2 * n_g + h)),
            pl.BlockSpec((1, w), lambda h: (0, h)),
        ],
        out_specs=pl.BlockSpec((META_PAD, w), lambda h: (0, h)),
        out_shape=jax.ShapeDtypeStruct((META_PAD, SB_WIDTH), BF16),
        scratch_shapes=_sb_scratch(META_PAD),
        compiler_params=_cparams(("arbitrary",)),
        name="sb_attention_meta",
    )(ps_meta, ps_meta, ps_meta, norm_g)


def _bd(x, seg_ones):
    return jnp.concatenate([x.astype(BF16)] * HEADS_PER_GROUP, axis=0) * seg_ones


def _seg_sum(a, seg_ones):
    return _dot(a.astype(BF16), seg_ones)


def _mm_bd(a, b, seg_ones):
    return _dot(a.astype(BF16), _bd(b, seg_ones))


def _rwkv_kernel(pm_ref, pl_ref, prev_m_ref, prev_l_ref, s0_ref, mu_m_ref, mu_l_ref,
                 w0_ref, w2_ref, a0_ref, a2_ref, g2_ref, kk_ref, ka_ref, rk_ref, gng_ref, gnb_ref,
                 y_ref, s_out_ref, s_ref, carry_m_ref, carry_l_ref, *, n_valid):
    c = pl.program_id(1)
    L = CHUNK
    nb = pm_ref.shape[0]

    @pl.when(c == 0)
    def _():
        for bi in range(nb):
            s_ref[bi * N_GROUPS:(bi + 1) * N_GROUPS] = s0_ref[...]
            carry_m_ref[bi] = prev_m_ref[...]
            carry_l_ref[bi] = prev_l_ref[...]

    row = lax.broadcasted_iota(jnp.int32, (L, 1), 0)
    first = row == 0

    def token_shift(p, prev_last, mu):
        shifted = jnp.where(first, prev_last, pltpu.roll(p, 1, 0))
        return p + (shifted - p) * mu

    r4 = lax.broadcasted_iota(jnp.int32, (4 * L, GROUP), 0)
    c4 = lax.broadcasted_iota(jnp.int32, (4 * L, GROUP), 1)
    bd_mask = (r4 >> LOG2_CHUNK) == (c4 >> LOG2_HEAD)
    seg_ones = bd_mask.astype(BF16)
    ri = lax.broadcasted_iota(jnp.int32, (L, L), 0)
    ci = lax.broadcasted_iota(jnp.int32, (L, L), 1)
    tri_incl = (ci <= ri).astype(BF16)
    r2 = lax.broadcasted_iota(jnp.int32, (2 * L, 4 * L), 0)
    c2 = lax.broadcasted_iota(jnp.int32, (2 * L, 4 * L), 1) & (L - 1)
    low_mask = c2 < (r2 & (L - 1)) + (r2 >> LOG2_CHUNK)
    rl = lax.broadcasted_iota(jnp.int32, (L, 4 * L), 0)
    cl = lax.broadcasted_iota(jnp.int32, (L, 4 * L), 1) & (L - 1)
    eye_t = (rl == cl).astype(F32)
    consts = (row, first, bd_mask, seg_ones, tri_incl, low_mask, eye_t)

    gsl = [slice(g * GROUP, (g + 1) * GROUP) for g in range(N_GROUPS)]
    xm, lora = [], ([], [], [])
    for bi in range(nb):
        pl_blk = pl_ref[bi]
        xl = token_shift(pl_blk, carry_l_ref[bi, 7:8, :], mu_l_ref[...])
        carry_l_ref[bi] = pl_blk[L - 8:, :]
        parts = (jnp.tanh(xl[:, 0:LORA_W]).astype(BF16), xl[:, LORA_W:2 * LORA_W].astype(BF16),
                 _sigmoid(xl[:, 2 * LORA_W:]).astype(BF16))
        pm_blk = pm_ref[bi]
        xm_all = token_shift(pm_blk, carry_m_ref[bi, 7:8, :], mu_m_ref[...])
        carry_m_ref[bi] = pm_blk[L - 8:, :]
        xm += [xm_all[:, g * 3 * GROUP:(g + 1) * 3 * GROUP] for g in range(N_GROUPS)]
        for dst, part in zip(lora, parts):
            dst += [part] * N_GROUPS
    prm = [[ref[:, gs] for gs in gsl] for ref in
           (w0_ref, w2_ref, a0_ref, a2_ref, g2_ref, kk_ref, ka_ref, rk_ref, gng_ref, gnb_ref)]
    n_chains = nb * N_GROUPS
    ys, s_new = _rwkv_groups(xm, lora, [p * nb for p in prm], [s_ref[i] for i in range(n_chains)],
                             consts, n_valid)
    for i in range(n_chains):
        s_ref[i] = s_new[i]
        s_out_ref[i] = s_new[i]
        y_ref[i // N_GROUPS, :, gsl[i % N_GROUPS]] = ys[i].astype(y_ref.dtype)


def _rwkv_groups(xm, lora, prm, s_t, consts, n_valid):
    L = CHUNK
    row, first, bd_mask, seg_ones, tri_incl, low_mask, eye_t = consts
    w0, w2, a0, a2, g2, k_k, k_a, r_k, gn_g, gn_b = prm
    tanh_wl, al, sig_gl = lora
    r = [x[:, 0:GROUP] for x in xm]
    k = [x[:, GROUP:2 * GROUP] for x in xm]
    v = [x[:, 2 * GROUP:3 * GROUP] for x in xm]

    dw = _each(lambda b, x, w: b + _dot(x, w), w0, tanh_wl, w2)
    w_log = _each(lambda d: jnp.minimum(d, 0.0) - jnp.log(1.0 + jnp.exp(_neg_abs(d))) - 0.5, dw)
    logw = _each(lambda x: -jnp.exp(x), w_log)
    iclr = _each(lambda b, x, w: _sigmoid(b + _dot(x, w)), a0, al, a2)
    gate = _each(_dot, sig_gl, g2)

    kk = _each(lambda a, b: a * b, k, k_k)
    kk = _each(lambda x: x * lax.rsqrt(jnp.maximum(_seg_sum(x * x, seg_ones), 1e-24)), kk)
    kmod = _each(lambda x, i, a: x * (1.0 + (i - 1.0) * a), k, iclr, k_a)

    if n_valid < L:
        valid = row < n_valid
        zero_pad = lambda x: jnp.where(valid, x, 0.0)
        kk, kmod, v, logw = _each(zero_pad, kk), _each(zero_pad, kmod), _each(zero_pad, v), _each(zero_pad, logw)

    cum = _each(lambda x: _mm_hi_lhs_const(tri_incl, x), logw)
    p_in = _each(jnp.exp, cum)
    p_ex = _each(lambda c, w: jnp.exp(c - w), cum, logw)
    p_inv = _each(lambda c: jnp.exp(-c), cum)
    p_tail = _each(lambda c: jnp.exp(c[L - 1:L, :] - c), cum)
    p_last = _each(lambda c: jnp.exp(c[L - 1:L, :]), cum)
    at = _each(lambda a, p: -a * p, kk, p_ex)
    kb = _each(lambda a, i: a * i, kk, iclr)
    bt = _each(lambda a, p: (a * p).astype(BF16), kb, p_inv)
    kt = _each(lambda a, p: (a * p).astype(BF16), kmod, p_inv)
    rt = _each(lambda a, p: a * p, r, p_in)

    ar = _each(lambda a, b: jnp.concatenate([a, b], axis=0).astype(BF16), at, rt)
    abr = _each(lambda a, b: jnp.where(low_mask, _dot_nt(a, _bd(b, seg_ones)), 0.0), ar, bt)
    akr = _each(lambda a, b: jnp.where(low_mask, _dot_nt(a, _bd(b, seg_ones)), 0.0), ar, kt)
    sa = _each(lambda a, s: _dot_nt(a, s.astype(BF16)), ar, s_t)
    bd_v = _each(lambda x: _bd(x, seg_ones), v)
    x = _each(lambda s, a, b: s[:L] + _dot(a[:L].astype(BF16), b), sa, akr, bd_v)

    acc = [eye_t] * len(xm)
    pw = [a[:L] for a in abr]
    for lvl in range(LOG2_CHUNK):
        if lvl < LOG2_CHUNK - 1:
            res = _each(lambda a, p: _mm_bd(jnp.concatenate([a, p], axis=0), p, seg_ones), acc, pw)
            acc = _each(lambda a, rr: a + rr[:L], acc, res)
            pw = [rr[L:] for rr in res]
        else:
            acc = _each(lambda a, p: a + _mm_bd(a, p, seg_ones), acc, pw)
    u = _each(lambda a, b: _mm_bd(a, b, seg_ones), acc, x)

    rbk = _each(lambda a, b: jnp.concatenate([a[L:], b[L:]], axis=1).astype(BF16), abr, akr)
    uv_bd = _each(lambda a, b: jnp.concatenate([_bd(a, seg_ones), b], axis=0), u, bd_v)
    y = _each(lambda s, a, b: s[L:] + _dot(a, b), sa, rbk, uv_bd)

    uv_t = _each(lambda a, b: jnp.transpose(jnp.concatenate([a, b], axis=0)).astype(BF16), u, v)
    bk_tail = _each(lambda a, b, p: jnp.concatenate([a * p, b * p], axis=0).astype(BF16), kb, kmod, p_tail)
    s_new = _each(lambda s, p, a, b: s * p + jnp.where(bd_mask, _dot(a, b), 0.0), s_t, p_last, uv_t, bk_tail)

    mean = _each(lambda a: _seg_sum(a, seg_ones) * (1.0 / HEAD_DIM), y)
    yc = _each(lambda a, m: a - m, y, mean)
    var = _each(lambda a: _seg_sum(a * a, seg_ones) * (1.0 / HEAD_DIM), yc)
    yn = _each(lambda a, s, g_, b_: a * lax.rsqrt(s + GN_EPS) * g_ + b_, yc, var, gn_g, gn_b)
    bonus = _each(lambda a, b, c, d: _seg_sum(a * b * c, seg_ones) * d, r, kmod, r_k, v)
    out = _each(lambda a, b, c: (a + b) * c, yn, bonus, gate)
    return out, s_new


def _mm_hi_lhs_const(a_bf16, b):
    hi, lo = _split(b)
    return _dot(a_bf16, hi) + _dot(a_bf16, lo)


def _rwkv(p, prev, prev_blk, s0, prm, nb, n_valid):
    batch, rows, _ = p.shape
    n_chunks = rows // CHUNK
    n_main = 3 * RWKV_WIDTH
    lora_blk = n_main // LORA_COLS
    row_map = lambda b, c: (b, c, 0)
    whole = lambda b, c: (0, 0)
    vec = pl.BlockSpec((1, RWKV_WIDTH), whole)
    kern = functools.partial(_rwkv_kernel, n_valid=n_valid)
    return pl.pallas_call(
        kern,
        grid=(batch // nb, n_chunks),
        in_specs=[
            pl.BlockSpec((nb, CHUNK, n_main), row_map),
            pl.BlockSpec((nb, CHUNK, LORA_COLS), lambda b, c: (b, c, lora_blk)),
            pl.BlockSpec((8, n_main), lambda b, c: (prev_blk, 0)),
            pl.BlockSpec((8, LORA_COLS), lambda b, c: (prev_blk, lora_blk)),
            pl.BlockSpec((N_GROUPS, GROUP, GROUP), lambda b, c: (0, 0, 0)),
            pl.BlockSpec((1, n_main), whole),
            pl.BlockSpec((1, LORA_COLS), whole),
            vec,
            pl.BlockSpec((LORA_W, RWKV_WIDTH), whole),
            vec,
            pl.BlockSpec((LORA_W, RWKV_WIDTH), whole),
            pl.BlockSpec((LORA_G, RWKV_WIDTH), whole),
            vec, vec, vec, vec, vec,
        ],
        out_specs=[
            pl.BlockSpec((nb, CHUNK, RWKV_WIDTH), row_map),
            pl.BlockSpec((nb * N_GROUPS, GROUP, GROUP), lambda b, c: (b, 0, 0)),
        ],
        out_shape=[
            jax.ShapeDtypeStruct((batch, rows, RWKV_WIDTH), BF16),
            jax.ShapeDtypeStruct((batch * N_GROUPS, GROUP, GROUP), F32),
        ],
        scratch_shapes=[
            pltpu.VMEM((nb * N_GROUPS, GROUP, GROUP), F32),
            pltpu.VMEM((nb, 8, n_main), F32),
            pltpu.VMEM((nb, 8, LORA_COLS), F32),
        ],
        compiler_params=_cparams(("arbitrary", "arbitrary")),
        name="rwkv7",
    )(p, p, prev, prev, s0, prm["mu_m"], prm["mu_l"], prm["w0"], prm["w2"], prm["a0"], prm["a2"],
      prm["g2"], prm["k_k"], prm["k_a"], prm["r_k"], prm["gn_g"], prm["gn_b"])


def _outproj_ln_kernel(yr_ref, ys_ref, x_ref, w_ref, g0_ref, b0_ref, g_ref, b_ref, o_ref):
    h = _layer_norm(x_ref[...], g0_ref[...], b0_ref[...])
    mix = _dot(yr_ref[...], w_ref[0:RWKV_WIDTH, :]) + _dot(ys_ref[...], w_ref[RWKV_WIDTH:, :])
    o_ref[...] = _layer_norm(DEEPNORM_ALPHA * h + mix, g_ref[...], b_ref[...])


def _outproj_ln(yr, ys, x, w, g0, b0, g, b, tm):
    n = x.shape[0]
    return pl.pallas_call(
        _outproj_ln_kernel,
        grid=(n // tm,),
        in_specs=[
            pl.BlockSpec((tm, RWKV_WIDTH), lambda i: (i, 0)),
            pl.BlockSpec((tm, SB_WIDTH), lambda i: (i, 0)),
            pl.BlockSpec((tm, D_MODEL), lambda i: (i, 0)),
            pl.BlockSpec((D_MODEL, D_MODEL), lambda i: (0, 0)),
            pl.BlockSpec((1, D_MODEL), lambda i: (0, 0)),
            pl.BlockSpec((1, D_MODEL), lambda i: (0, 0)),
            pl.BlockSpec((1, D_MODEL), lambda i: (0, 0)),
            pl.BlockSpec((1, D_MODEL), lambda i: (0, 0)),
        ],
        out_specs=pl.BlockSpec((tm, D_MODEL), lambda i: (i, 0)),
        out_shape=jax.ShapeDtypeStruct((n, D_MODEL), F32),
        compiler_params=_cparams(("arbitrary",)),
        name="outproj_ln",
    )(yr, ys, x, w, g0, b0, g, b)


FF_TF = 512
FF_NF = D_FF // FF_TF


def _gate_halo_kernel(h_ref, w_ref, o_ref):
    o_ref[0] = _dot(h_ref[...].astype(BF16), w_ref[...])


def _gate_halo(h_rows, w_up):
    return pl.pallas_call(
        _gate_halo_kernel,
        grid=(FF_NF,),
        in_specs=[
            pl.BlockSpec((8, D_MODEL), lambda f: (N_META // 8 - 1, 0)),
            pl.BlockSpec((D_MODEL, FF_TF), lambda f: (0, f)),
        ],
        out_specs=pl.BlockSpec((1, 8, FF_TF), lambda f: (f, 0, 0)),
        out_shape=jax.ShapeDtypeStruct((FF_NF, 8, FF_TF), F32),
        compiler_params=_cparams(("arbitrary",)),
        name="ffn_gate_halo",
    )(h_rows, w_up)


def _ffn_kernel(h_ref, halo0_ref, wg_ref, wv_ref, cw_ref, cb_ref, wd_ref, g_ref, b_ref, o_ref,
                hb_ref, acc_ref, halo_ref, *, tiles_per_seq):
    i = pl.program_id(0)
    f = pl.program_id(1)
    tm = h_ref.shape[0]

    @pl.when(f == 0)
    def _():
        hb_ref[...] = h_ref[...].astype(BF16)
        acc_ref[...] = jnp.zeros_like(acc_ref)

    @pl.when((i % tiles_per_seq) == 0)
    def _():
        halo_ref[f] = halo0_ref[0]

    hb = hb_ref[...]
    row = lax.broadcasted_iota(jnp.int32, (tm, 1), 0)
    halves = [slice(s * MXU_DIM, (s + 1) * MXU_DIM) for s in range(FF_TF // MXU_DIM)]
    gate = [_dot(hb, wg_ref[:, cs]) for cs in halves]
    val = [_dot(hb, wv_ref[:, cs]) for cs in halves]
    down = []
    for cs, g, v in zip(halves, gate, val):
        halo = halo_ref[f, :, cs]
        halo_ref[f, :, cs] = g[tm - 8:, :]
        g1 = jnp.where(row == 0, halo[7:8, :], pltpu.roll(g, 1, 0))
        g2 = jnp.where(row == 0, halo[6:7, :], jnp.where(row == 1, halo[7:8, :], pltpu.roll(g, 2, 0)))
        conv = cw_ref[0:1, cs] * g2 + cw_ref[1:2, cs] * g1 + cw_ref[2:3, cs] * g + cb_ref[:, cs]
        act = (conv * _sigmoid(conv) * v).astype(BF16)
        down.append(_dot(act, wd_ref[cs, :]))
    acc_ref[...] += functools.reduce(lambda a, b: a + b, down)

    @pl.when(f == FF_NF - 1)
    def _():
        o_ref[...] = _layer_norm(DEEPNORM_ALPHA * h_ref[...] + acc_ref[...], g_ref[...], b_ref[...])


def _ffn(h, halo0, w_up, conv_w, conv_b, w_down, g, b, tm, seq):
    n = h.shape[0]
    kern = functools.partial(_ffn_kernel, tiles_per_seq=seq // tm)
    return pl.pallas_call(
        kern,
        grid=(n // tm, FF_NF),
        in_specs=[
            pl.BlockSpec((tm, D_MODEL), lambda i, f: (i, 0)),
            pl.BlockSpec((1, 8, FF_TF), lambda i, f: (f, 0, 0)),
            pl.BlockSpec((D_MODEL, FF_TF), lambda i, f: (0, f)),
            pl.BlockSpec((D_MODEL, FF_TF), lambda i, f: (0, FF_NF + f)),
            pl.BlockSpec((3, FF_TF), lambda i, f: (0, f)),
            pl.BlockSpec((1, FF_TF), lambda i, f: (0, f)),
            pl.BlockSpec((FF_TF, D_MODEL), lambda i, f: (f, 0)),
            pl.BlockSpec((1, D_MODEL), lambda i, f: (0, 0)),
            pl.BlockSpec((1, D_MODEL), lambda i, f: (0, 0)),
        ],
        out_specs=pl.BlockSpec((tm, D_MODEL), lambda i, f: (i, 0)),
        out_shape=jax.ShapeDtypeStruct((n, D_MODEL), F32),
        scratch_shapes=[
            pltpu.VMEM((tm, D_MODEL), BF16),
            pltpu.VMEM((tm, D_MODEL), F32),
            pltpu.VMEM((FF_NF, 8, FF_TF), F32),
        ],
        compiler_params=_cparams(("arbitrary", "arbitrary")),
        name="conv_ffn",
    )(h, halo0, w_up, w_up, conv_w, conv_b, w_down, g, b)


def _group_major(a):
    parts = [a[..., s * RWKV_WIDTH:(s + 1) * RWKV_WIDTH] for s in range(3)]
    out = []
    for gi in range(N_GROUPS):
        out += [p[..., gi * GROUP:(gi + 1) * GROUP] for p in parts]
    return jnp.concatenate(out, axis=-1)


def _pad_cols(a, width):
    return jnp.pad(a, [(0, 0)] * (a.ndim - 1) + [(0, width - a.shape[-1])])


def _pad_rows(a, rows):
    return jnp.pad(a, [(0, rows - a.shape[0])] + [(0, 0)] * (a.ndim - 1))


def _lora_slots(a):
    c = 3 * RWKV_WIDTH
    wl = a[..., c:c + DECAY_LORA]
    al = a[..., c + DECAY_LORA:c + DECAY_LORA + ICLR_LORA]
    gl = a[..., c + DECAY_LORA + ICLR_LORA:c + DECAY_LORA + ICLR_LORA + GATE_LORA]
    return jnp.concatenate([_pad_cols(wl, LORA_W), _pad_cols(al, LORA_W), _pad_cols(gl, LORA_G)], axis=-1)


def kernel(x, meta_tokens, emb_ln_g, emb_ln_b, w_in, rwkv_mu, rwkv_w0, rwkv_w2, rwkv_a0, rwkv_a2, rwkv_g2, rwkv_k_k, rwkv_k_a, rwkv_r_k, rwkv_gn_g, rwkv_gn_b, sb_norm_g, w_out, ln1_g, ln1_b, ffn_w_up, ffn_conv_w, ffn_conv_b, ffn_w_down, ln2_g, ln2_b):
    batch, seq, d = x.shape
    assert d == D_MODEL and seq % 512 == 0 and w_in.shape[0] == 1 and batch % RWKV_BATCH_PER_STEP == 0
    n_rw = 3 * RWKV_WIDTH + DECAY_LORA + ICLR_LORA + GATE_LORA
    row2 = lambda a: a.reshape(1, -1)
    grp3 = lambda a: a.reshape(N_GROUPS, 1, GROUP)

    wi = w_in[0].astype(BF16)
    sb_cols = wi[:, n_rw:]
    sb_cols = jnp.concatenate([sb_cols[:, :SB_WIDTH] * (HEAD_DIM ** -0.5), sb_cols[:, SB_WIDTH:]], axis=1)
    w1 = jnp.concatenate([_group_major(wi[:, :3 * RWKV_WIDTH]), _lora_slots(wi), sb_cols], axis=1)
    mu = rwkv_mu[0]
    prm = {
        "mu_m": row2(_group_major(mu[:3 * RWKV_WIDTH])),
        "mu_l": row2(_lora_slots(mu)),
        "w0": row2(rwkv_w0[0]),
        "w2": _pad_rows(rwkv_w2[0], LORA_W).astype(BF16),
        "a0": row2(rwkv_a0[0]),
        "a2": _pad_rows(rwkv_a2[0], LORA_W).astype(BF16),
        "g2": _pad_rows(rwkv_g2[0], LORA_G).astype(BF16),
        "k_k": row2(rwkv_k_k[0]),
        "k_a": row2(rwkv_k_a[0]),
        "r_k": row2(rwkv_r_k[0]),
        "gn_g": row2(rwkv_gn_g[0]),
        "gn_b": row2(rwkv_gn_b[0]),
    }
    w_o = w_out[0].astype(BF16)
    w_up = ffn_w_up[0].astype(BF16)
    w_dn = ffn_w_down[0].astype(BF16)
    g0, b0 = row2(emb_ln_g), row2(emb_ln_b)
    g1, b1 = row2(ln1_g[0]), row2(ln1_b[0])
    g2, b2 = row2(ln2_g[0]), row2(ln2_b[0])
    sbg = row2(sb_norm_g[0])

    xm = _pad_rows(meta_tokens.astype(x.dtype), META_PAD)
    pr_m, ps_m = _ln_inproj(xm, g0, b0, w1, META_PAD)
    ysb_m = _sb_attention_meta(ps_m, sbg)
    zeros_state = jnp.zeros((N_GROUPS, GROUP, GROUP), F32)
    yr_m, s_meta = _rwkv(pr_m[:CHUNK].reshape(1, CHUNK, RWKV_PCOLS), jnp.zeros((8, RWKV_PCOLS), F32), 0,
                         zeros_state, prm, 1, N_META)
    h1_m = _outproj_ln(_pad_rows(yr_m[0], META_PAD), ysb_m, xm, w_o, g0, b0, g1, b1, META_PAD)
    halo0 = _gate_halo(h1_m, w_up)

    xr = x.reshape(batch * seq, d)
    pr, ps = _ln_inproj(xr, g0, b0, w1, 1024)
    y_sb = _sb_attention(ps, ps_m, sbg, batch, seq)
    y_rw, _ = _rwkv(pr.reshape(batch, seq, RWKV_PCOLS), pr_m, N_META // 8 - 1, s_meta, prm,
                    RWKV_BATCH_PER_STEP, CHUNK)
    h1 = _outproj_ln(y_rw.reshape(batch * seq, RWKV_WIDTH), y_sb, xr, w_o, g0, b0, g1, b1, 512)
    out = _ffn(h1, halo0, w_up, ffn_conv_w[0], row2(ffn_conv_b[0]), w_dn, g2, b2, 512, seq)
    return out.reshape(batch, seq, d)
```
